```python
import math
import jax, jax.numpy as jnp
from jax import lax
import numpy as np

D_MODEL = 1024
BATCH = 2
SEQ = 8192
DEPTH = 2

MEM_LEN = 256
HG_HEADS = 8
HG_KDIM = 128
HG_VDIM = 128
HG_QK_W = HG_HEADS * HG_KDIM
HG_WIDTH = HG_HEADS * HG_VDIM
HG_CHUNK = 64
MLA_HEADS = 8
MLA_NOPE = 64
MLA_ROPE = 32
MLA_VDIM = 64
MLA_WIDTH = MLA_HEADS * MLA_VDIM
MLA_Q_RANK = 256
MLA_KV_RANK = 128
ROPE_THETA = 10000.0
Q_BLOCK = 128
MEM_HEADS = 4
MEM_HDIM = 128
MEM_WIDTH = MEM_HEADS * MEM_HDIM
MIX_WIDTH = HG_WIDTH + MLA_WIDTH + MEM_WIDTH
IN_SIZES = (HG_QK_W, HG_QK_W, HG_WIDTH, HG_WIDTH,
            MLA_Q_RANK, MLA_KV_RANK, MLA_ROPE, MLA_WIDTH,
            MEM_WIDTH, MEM_WIDTH)
IN_WIDTH = sum(IN_SIZES)
IN_OFFSETS = tuple(int(v) for v in np.cumsum(IN_SIZES)[:-1])
DEEPNORM_ALPHA = (2 * DEPTH) ** 0.25
DEEPNORM_BETA = (8 * DEPTH) ** -0.25
RMS_EPS = 1e-6
LN_EPS = 1e-5
NEG_BIG = -1e30

kernel_name = 'hymba_hgrn2_mla_mem_deepnorm'


def _rmsnorm(t, g):
    t32 = t.astype(jnp.float32)
    t32 = t32 * lax.rsqrt(jnp.mean(t32 * t32, axis=-1, keepdims=True) + RMS_EPS)
    return t32.astype(t.dtype) * g


def _layernorm(t, g, b):
    t32 = t.astype(jnp.float32)
    mu = jnp.mean(t32, axis=-1, keepdims=True)
    var = jnp.mean(jnp.square(t32 - mu), axis=-1, keepdims=True)
    return ((t32 - mu) * lax.rsqrt(var + LN_EPS)).astype(t.dtype) * g + b


def _rope(t, cos, sin):
    half = t.shape[-1] // 2
    t1, t2 = t[..., :half], t[..., half:]
    return jnp.concatenate([t1 * cos - t2 * sin, t1 * sin + t2 * cos], axis=-1)


def _chunk_gated_recurrence(q, k, v, logf):
    B, S, H, dk = q.shape
    dv = v.shape[-1]
    C = HG_CHUNK
    n = S // C

    def to_chunks(t):
        return t.reshape(B, n, C, H, t.shape[-1]).transpose(1, 0, 3, 2, 4)

    tri = jnp.tril(jnp.ones((C, C), dtype=jnp.float32))

    def step(state, inp):
        qc, kc, vc, gc = inp
        b = jnp.cumsum(gc, axis=2)
        diff = b[:, :, :, None, :] - b[:, :, None, :, :]
        decay = jnp.exp(jnp.minimum(diff, 0.0)) * tri[:, :, None]
        A = jnp.einsum('bhtsd,bhsd->bhts', qc[:, :, :, None, :] * decay, kc)
        o = (jnp.einsum('bhts,bhsv->bhtv', A, vc)
             + jnp.einsum('bhtd,bhdv->bhtv', qc * jnp.exp(b), state))
        b_last = b[:, :, -1:, :]
        state = (jnp.exp(b_last[:, :, 0, :])[..., None] * state
                 + jnp.einsum('bhsd,bhsv->bhdv', kc * jnp.exp(b_last - b), vc))
        return state, o

    state0 = jnp.zeros((B, H, dk, dv), jnp.float32)
    _, o = lax.scan(step, state0, (to_chunks(q), to_chunks(k), to_chunks(v), to_chunks(logf)))
    return o.transpose(1, 0, 3, 2, 4).reshape(B, S, H, dv)


def _hgrn2_group(q, f_pre, i, lb, out_g):
    B, S, _ = q.shape
    q32 = q.reshape(B, S, HG_HEADS, HG_KDIM).astype(jnp.float32)
    z = f_pre.reshape(B, S, HG_HEADS, HG_KDIM).astype(jnp.float32)
    lb = lb.reshape(HG_HEADS, HG_KDIM)
    logf = jax.nn.log_sigmoid(z) + jnp.log1p(lb * jnp.exp(-z))
    k = (1.0 - lb) * jax.nn.sigmoid(-z)
    v = i.reshape(B, S, HG_HEADS, HG_VDIM).astype(jnp.float32)
    o = _chunk_gated_recurrence(q32, k, v, logf)
    o = o * lax.rsqrt(jnp.mean(o * o, axis=-1, keepdims=True) + RMS_EPS)
    return (o.astype(q.dtype) * out_g).reshape(B, S, HG_WIDTH)


def _mla_group(cq, ckv, kr, q_norm, kv_norm, w_uq, w_uk, w_uv, cos, sin):
    B, S, _ = cq.shape
    q = (_rmsnorm(cq, q_norm) @ w_uq).reshape(B, S, MLA_HEADS, MLA_NOPE + MLA_ROPE)
    q_nope = q[..., :MLA_NOPE]
    q_rope = _rope(q[..., MLA_NOPE:], cos[:, :, None, :], sin[:, :, None, :])
    c = _rmsnorm(ckv, kv_norm)
    k_nope = (c @ w_uk).reshape(B, S, MLA_HEADS, MLA_NOPE)
    v = (c @ w_uv).reshape(B, S, MLA_HEADS, MLA_VDIM)
    k_rope = _rope(kr, cos, sin)
    scale = 1.0 / math.sqrt(MLA_NOPE + MLA_ROPE)
    nb = S // Q_BLOCK
    qn_b = q_nope.reshape(B, nb, Q_BLOCK, MLA_HEADS, MLA_NOPE).transpose(1, 0, 2, 3, 4)
    qr_b = q_rope.reshape(B, nb, Q_BLOCK, MLA_HEADS, MLA_ROPE).transpose(1, 0, 2, 3, 4)
    key_idx = jnp.arange(S)

    def attend(args):
        qn, qr, blk = args
        s = (jnp.einsum('bqhd,bkhd->bhqk', qn, k_nope)
             + jnp.einsum('bqhr,bkr->bhqk', qr, k_rope)).astype(jnp.float32) * scale
        q_idx = blk * Q_BLOCK + jnp.arange(Q_BLOCK)
        mask = key_idx[None, :] <= q_idx[:, None]
        s = jnp.where(mask, s, NEG_BIG)
        p = jax.nn.softmax(s, axis=-1).astype(v.dtype)
        return jnp.einsum('bhqk,bkhd->bqhd', p, v)

    o = lax.map(attend, (qn_b, qr_b, jnp.arange(nb)))
    return o.transpose(1, 0, 2, 3, 4).reshape(B, S, MLA_WIDTH)


def _memory_group(qm, mem, w_k, w_v):
    B, S, _ = qm.shape
    M = mem.shape[1]
    q = qm.reshape(B, S, MEM_HEADS, MEM_HDIM)
    k = (mem @ w_k).reshape(B, M, MEM_HEADS, MEM_HDIM)
    v = (mem @ w_v).reshape(B, M, MEM_HEADS, MEM_HDIM)
    s = jnp.einsum('bshd,bmhd->bhsm', q, k).astype(jnp.float32) / math.sqrt(MEM_HDIM)
    p = jax.nn.softmax(s, axis=-1).astype(v.dtype)
    return jnp.einsum('bhsm,bmhd->bshd', p, v).reshape(B, S, MEM_WIDTH)


def setup_inputs(seed: int = 0) -> dict:
    key = jax.random.key(seed)
    ks = jax.random.split(key, 16)
    nrm = jax.random.normal
    x = nrm(ks[0], (BATCH, SEQ, D_MODEL), jnp.float32)
    mem = nrm(ks[1], (BATCH, MEM_LEN, D_MODEL), jnp.float32)
    offsets = jax.random.randint(ks[2], (BATCH, 1), 0, 1024, dtype=jnp.int32)
    positions = (jnp.arange(SEQ, dtype=jnp.int32)[None, :] + offsets).astype(jnp.int32)
    w_in = nrm(ks[3], (DEPTH, D_MODEL, IN_WIDTH)) * D_MODEL ** -0.5
    hgrn_lb_logits = 0.1 * nrm(ks[4], (DEPTH, HG_QK_W))
    hgrn_out_norm = 1.0 + 0.02 * nrm(ks[5], (DEPTH, HG_VDIM))
    mla_q_norm = 1.0 + 0.02 * nrm(ks[6], (DEPTH, MLA_Q_RANK))
    mla_kv_norm = 1.0 + 0.02 * nrm(ks[7], (DEPTH, MLA_KV_RANK))
    w_mla_uq = nrm(ks[8], (DEPTH, MLA_Q_RANK, MLA_HEADS * (MLA_NOPE + MLA_ROPE))) * MLA_Q_RANK ** -0.5
    w_mla_uk = nrm(ks[9], (DEPTH, MLA_KV_RANK, MLA_HEADS * MLA_NOPE)) * MLA_KV_RANK ** -0.5
    w_mla_uv = nrm(ks[10], (DEPTH, MLA_KV_RANK, MLA_WIDTH)) * (MLA_KV_RANK ** -0.5 * DEEPNORM_BETA)
    w_mem_k = nrm(ks[11], (DEPTH, D_MODEL, MEM_WIDTH)) * D_MODEL ** -0.5
    w_mem_v = nrm(ks[12], (DEPTH, D_MODEL, MEM_WIDTH)) * (D_MODEL ** -0.5 * DEEPNORM_BETA)
    w_out = nrm(ks[13], (DEPTH, MIX_WIDTH, D_MODEL)) * (MIX_WIDTH ** -0.5 * DEEPNORM_BETA)
    ln_g = 1.0 + 0.02 * nrm(ks[14], (DEPTH, D_MODEL))
    ln_b = 0.02 * nrm(ks[15], (DEPTH, D_MODEL))
    return {'x': x, 'mem': mem, 'positions': positions, 'w_in': w_in,
            'hgrn_lb_logits': hgrn_lb_logits, 'hgrn_out_norm': hgrn_out_norm,
            'mla_q_norm': mla_q_norm, 'mla_kv_norm': mla_kv_norm,
            'w_mla_uq': w_mla_uq, 'w_mla_uk': w_mla_uk, 'w_mla_uv': w_mla_uv,
            'w_mem_k': w_mem_k, 'w_mem_v': w_mem_v, 'w_out': w_out,
            'ln_g': ln_g, 'ln_b': ln_b}


def reference(x, mem, positions, w_in, hgrn_lb_logits, hgrn_out_norm, mla_q_norm,
              mla_kv_norm, w_mla_uq, w_mla_uk, w_mla_uv, w_mem_k, w_mem_v, w_out,
              ln_g, ln_b):
    inv_freq = 1.0 / (ROPE_THETA ** (jnp.arange(0, MLA_ROPE, 2, dtype=jnp.float32) / MLA_ROPE))
    ang = positions.astype(jnp.float32)[..., None] * inv_freq
    cos = jnp.cos(ang).astype(x.dtype)
    sin = jnp.sin(ang).astype(x.dtype)
    lb_soft = jax.nn.softmax(hgrn_lb_logits.astype(jnp.float32), axis=0)
    lower_bounds = jnp.cumsum(lb_soft, axis=0) - lb_soft[0]

    for l in range(DEPTH):
        h = x @ w_in[l]
        (hg_q, hg_f, hg_i, hg_gate, mla_cq, mla_ckv, mla_kr, mla_gate,
         mem_q, mem_gate) = jnp.split(h, IN_OFFSETS, axis=-1)
        y_hg = _hgrn2_group(hg_q, hg_f, hg_i, lower_bounds[l], hgrn_out_norm[l])
        y_mla = _mla_group(mla_cq, mla_ckv, mla_kr, mla_q_norm[l], mla_kv_norm[l],
                           w_mla_uq[l], w_mla_uk[l], w_mla_uv[l], cos, sin)
        y_mem = _memory_group(mem_q, mem, w_mem_k[l], w_mem_v[l])
        y = jnp.concatenate([y_hg * jax.nn.silu(hg_gate),
                             y_mla * jax.nn.silu(mla_gate),
                             y_mem * jax.nn.silu(mem_gate)], axis=-1)
        y = y @ w_out[l]
        x = _layernorm(DEEPNORM_ALPHA * x + y, ln_g[l], ln_b[l])
    return x
```

```python
import functools
import math

import jax
import jax.numpy as jnp
from jax import lax
from jax.experimental import pallas as pl
from jax.experimental.pallas import tpu as pltpu

F32 = jnp.float32
BF16 = jnp.bfloat16

LANES = 128

HG_HEADS = 8
HG_DIM = 128
HG_W = HG_HEADS * HG_DIM
HG_CHUNK = 64
HG_HALF = HG_CHUNK // 2
HG_QUARTER = HG_CHUNK // 4
HG_SAFE_DECAY = 40.0

MLA_HEADS = 8
MLA_NOPE = 64
MLA_ROPE = 32
MLA_QK = MLA_NOPE + MLA_ROPE
MLA_VDIM = 64
MLA_Q_RANK = 256
MLA_KV_RANK = 128
MLA_PAD = LANES
MLA_W = MLA_HEADS * MLA_VDIM

MEM_HEADS = 4
MEM_DIM = 128
MEM_W = MEM_HEADS * MEM_DIM

RMS_EPS = 1e-6
LN_EPS = 1e-5
NEG_BIG = -1e30

VMEM_LIMIT = 56 * 1024 * 1024


def _nt_dot(a, b):
    return lax.dot_general(a, b, (((1,), (1,)), ((), ())), preferred_element_type=F32)


def _tn_dot(a, b):
    return lax.dot_general(a, b, (((0,), (0,)), ((), ())), preferred_element_type=F32)


def _dot(a, b):
    return jnp.dot(a, b, preferred_element_type=F32)


def _silu(x):
    return x * (1.0 / (1.0 + jnp.exp(-x)))


def _hgrn_body(x_ref, w_ref, lb_ref, g_ref, y_ref,
               h_ref, q_ref, k_ref, b_ref, a_ref, st_ref, *, rows):
    n_chunk = rows // HG_CHUNK

    @pl.when(pl.program_id(1) == 0)
    def _reset_state():
        st_ref[...] = jnp.zeros_like(st_ref)

    x = x_ref[0]
    for g in range(4):
        cols = slice(g * HG_W, (g + 1) * HG_W)
        h_ref[:, cols] = _dot(x, w_ref[:, cols])

    for h in range(HG_HEADS):
        cols = slice(HG_W + h * HG_DIM, HG_W + (h + 1) * HG_DIM)
        z = h_ref[:, cols]
        lb = lb_ref[:, h * HG_DIM:(h + 1) * HG_DIM]
        log_sig = jnp.minimum(z, 0.0) - jnp.log1p(jnp.exp(-jnp.abs(z)))
        h_ref[:, cols] = log_sig + jnp.log1p(lb * jnp.exp(-z))
        k_ref[h] = (1.0 - lb) * (1.0 / (1.0 + jnp.exp(z)))
        q_ref[h] = h_ref[:, h * HG_DIM:(h + 1) * HG_DIM]

    r_i = lax.broadcasted_iota(jnp.int32, (HG_CHUNK, HG_CHUNK), 0)
    c_i = lax.broadcasted_iota(jnp.int32, (HG_CHUNK, HG_CHUNK), 1)
    causal = r_i >= c_i
    tri = jnp.where(causal, 1.0, 0.0).astype(BF16)
    worst = jnp.zeros((1, HG_W), F32)
    for j in range(n_chunk):
        rws = slice(j * HG_CHUNK, (j + 1) * HG_CHUNK)
        lf = h_ref[rws, HG_W:2 * HG_W]
        p0 = lf.astype(BF16)
        r0 = lf - p0.astype(F32)
        p1 = r0.astype(BF16)
        p2 = (r0 - p1.astype(F32)).astype(BF16)
        b = _dot(tri, p0) + _dot(tri, p1) + _dot(tri, p2)
        for h in range(HG_HEADS):
            b_ref[h, rws, :] = b[:, h * HG_DIM:(h + 1) * HG_DIM]
        q1 = b[HG_QUARTER - 1:HG_QUARTER]
        q2 = b[2 * HG_QUARTER - 1:2 * HG_QUARTER]
        q3 = b[3 * HG_QUARTER - 1:3 * HG_QUARTER]
        q4 = b[HG_CHUNK - 1:HG_CHUNK]
        worst = jnp.maximum(worst, jnp.maximum(jnp.maximum(-q1, q1 - q2),
                                               jnp.maximum(q2 - q3, q3 - q4)))
    unsafe = jnp.max(worst) > HG_SAFE_DECAY

    zeros_half = jnp.zeros((HG_HALF, HG_DIM), BF16)
    for j in range(n_chunk):
        rws = slice(j * HG_CHUNK, (j + 1) * HG_CHUNK)
        for h in range(HG_HEADS):
            bc = b_ref[h, rws, :]
            qc = q_ref[h, rws, :]
            kc = k_ref[h, rws, :]
            b_lo, b_hi = bc[:HG_HALF], bc[HG_HALF:]
            m0 = bc[HG_QUARTER - 1:HG_QUARTER]
            m1 = bc[3 * HG_QUARTER - 1:3 * HG_QUARTER]
            mid = bc[HG_HALF - 1:HG_HALF]
            qd0 = (qc[:HG_HALF] * jnp.exp(b_lo - m0)).astype(BF16)
            kd0 = (kc[:HG_HALF] * jnp.exp(m0 - b_lo)).astype(BF16)
            qd1 = (qc[HG_HALF:] * jnp.exp(b_hi - m1)).astype(BF16)
            kd1 = (kc[HG_HALF:] * jnp.exp(m1 - b_hi)).astype(BF16)
            qo = (qc[HG_HALF:] * jnp.exp(b_hi - mid)).astype(BF16)
            ko = (kc[:HG_HALF] * jnp.exp(mid - b_lo)).astype(BF16)
            q_cat = jnp.concatenate([
                jnp.concatenate([qd0, zeros_half], axis=0),
                jnp.concatenate([zeros_half, qd1], axis=0),
                jnp.concatenate([zeros_half, qo], axis=0)], axis=1)
            k_cat = jnp.concatenate([
                jnp.concatenate([kd0, zeros_half], axis=0),
                jnp.concatenate([zeros_half, kd1], axis=0),
                jnp.concatenate([ko, zeros_half], axis=0)], axis=1)
            a = _nt_dot(q_cat, k_cat)
            a_ref[j * HG_HEADS + h] = jnp.where(causal, a, 0.0)

    @pl.when(unsafe)
    def _direct_scores():
        def per_chunk_head(idx, carry):
            j = idx // HG_HEADS
            h = idx - j * HG_HEADS
            base = pl.multiple_of(j * HG_CHUNK, HG_CHUNK)
            bc = b_ref[h, pl.ds(base, HG_CHUNK), :]
            qc = q_ref[h, pl.ds(base, HG_CHUNK), :]

            def per_key(s, a):
                b_s = b_ref[h, pl.ds(base + s, 1), :]
                k_s = k_ref[h, pl.ds(base + s, 1), :]
                prod = qc * jnp.exp(jnp.minimum(bc - b_s, 0.0)) * k_s
                col = jnp.sum(prod, axis=-1, keepdims=True)
                return jnp.where((c_i == s) & (r_i >= s), col, a)

            a_ref[idx] = lax.fori_loop(0, HG_CHUNK, per_key,
                                       jnp.zeros((HG_CHUNK, HG_CHUNK), F32))
            return carry

        lax.fori_loop(0, n_chunk * HG_HEADS, per_chunk_head, 0)

    gain = g_ref[...]
    for j in range(n_chunk):
        rws = slice(j * HG_CHUNK, (j + 1) * HG_CHUNK)
        for h in range(HG_HEADS):
            hc = slice(h * HG_DIM, (h + 1) * HG_DIM)
            bc = b_ref[h, rws, :]
            qc = q_ref[h, rws, :]
            kc = k_ref[h, rws, :]
            vc = h_ref[rws, 2 * HG_W + h * HG_DIM:2 * HG_W + (h + 1) * HG_DIM].astype(BF16)
            gate = h_ref[rws, 3 * HG_W + h * HG_DIM:3 * HG_W + (h + 1) * HG_DIM]
            b_last = bc[HG_CHUNK - 1:HG_CHUNK]
            st = st_ref[h]
            q_hat = (qc * jnp.exp(bc)).astype(BF16)
            k_dec = (kc * jnp.exp(b_last - bc)).astype(BF16)
            o = _dot(a_ref[j * HG_HEADS + h].astype(BF16), vc) + _nt_dot(q_hat, st.astype(BF16))
            st_ref[h] = st * jnp.exp(b_last) + _tn_dot(vc, k_dec)
            o = o * lax.rsqrt(jnp.mean(o * o, axis=-1, keepdims=True) + RMS_EPS)
            y_ref[0, rws, hc] = (o * gain * _silu(gate)).astype(y_ref.dtype)


def _hgrn_call(xb, w_hg, lb, gain, *, rows):
    bsz, seq, d = xb.shape
    body = functools.partial(_hgrn_body, rows=rows)
    n_chunk = rows // HG_CHUNK
    return pl.pallas_call(
        body,
        grid=(bsz, seq // rows),
        in_specs=[
            pl.BlockSpec((1, rows, d), lambda b, t: (b, t, 0)),
            pl.BlockSpec((d, 4 * HG_W), lambda b, t: (0, 0)),
            pl.BlockSpec((1, HG_W), lambda b, t: (0, 0)),
            pl.BlockSpec((1, HG_DIM), lambda b, t: (0, 0)),
        ],
        out_specs=pl.BlockSpec((1, rows, HG_W), lambda b, t: (b, t, 0)),
        out_shape=jax.ShapeDtypeStruct((bsz, seq, HG_W), BF16),
        scratch_shapes=[
            pltpu.VMEM((rows, 4 * HG_W), F32),
            pltpu.VMEM((HG_HEADS, rows, HG_DIM), F32),
            pltpu.VMEM((HG_HEADS, rows, HG_DIM), F32),
            pltpu.VMEM((HG_HEADS, rows, HG_DIM), F32),
            pltpu.VMEM((n_chunk * HG_HEADS, HG_CHUNK, HG_CHUNK), F32),
            pltpu.VMEM((HG_HEADS, HG_DIM, HG_DIM), F32),
        ],
        compiler_params=pltpu.CompilerParams(
            dimension_semantics=("arbitrary", "arbitrary"), vmem_limit_bytes=VMEM_LIMIT),
        name="hgrn",
    )(xb, w_hg, lb, gain)


def _mla_prep_body(x_ref, w_ref, qn_ref, kvn_ref, wqa_ref, wqb_ref, wkt_ref, wv_ref,
                   ct_ref, st_ref, q_ref, kt_ref, v_ref, g_ref):
    x = x_ref[0]
    h = _dot(x, w_ref[...])
    cq = h[:, :MLA_Q_RANK]
    ckv = h[:, MLA_Q_RANK:MLA_Q_RANK + MLA_KV_RANK]
    o = MLA_Q_RANK + MLA_KV_RANK
    kr = h[:, o:o + MLA_PAD]
    kr_rot = h[:, o + MLA_PAD:o + 2 * MLA_PAD]
    gate = h[:, o + 2 * MLA_PAD:]
    ctab = ct_ref[0]
    stab = st_ref[0]

    cqn = (cq * lax.rsqrt(jnp.mean(cq * cq, axis=-1, keepdims=True) + RMS_EPS)
           * qn_ref[...]).astype(BF16)
    c = (ckv * lax.rsqrt(jnp.mean(ckv * ckv, axis=-1, keepdims=True) + RMS_EPS)
         * kvn_ref[...]).astype(BF16)

    scale = 1.0 / math.sqrt(MLA_QK)
    qa = _dot(cqn, wqa_ref[...])
    qb = _dot(cqn, wqb_ref[...])
    cq_tab = ctab * scale
    sq_tab = stab * scale
    for hd in range(MLA_HEADS):
        cols = slice(hd * MLA_PAD, (hd + 1) * MLA_PAD)
        q_ref[0, :, cols] = (qa[:, cols] * cq_tab + qb[:, cols] * sq_tab).astype(q_ref.dtype)

    kr_roped = (kr * ctab + kr_rot * stab).astype(BF16)
    ck = jnp.concatenate([c, kr_roped], axis=1)
    kt = _nt_dot(wkt_ref[...], ck)
    for hd in range(MLA_HEADS):
        kt_ref[0, hd, 0] = kt[hd * MLA_PAD:(hd + 1) * MLA_PAD].astype(kt_ref.dtype)

    v_ref[0] = _dot(c, wv_ref[...]).astype(v_ref.dtype)
    g_ref[0] = _silu(gate).astype(g_ref.dtype)


def _mla_prep_call(xb, w_b, qn, kvn, wqa, wqb, wkt, wv, ctab, stab, *, rows):
    bsz, seq, d = xb.shape
    n_blk = seq // rows
    hp = MLA_HEADS * MLA_PAD
    const = lambda b, t: (0, 0)
    return pl.pallas_call(
        _mla_prep_body,
        grid=(bsz, n_blk),
        in_specs=[
            pl.BlockSpec((1, rows, d), lambda b, t: (b, t, 0)),
            pl.BlockSpec(w_b.shape, const),
            pl.BlockSpec((1, MLA_Q_RANK), const),
            pl.BlockSpec((1, MLA_KV_RANK), const),
            pl.BlockSpec(wqa.shape, const),
            pl.BlockSpec(wqb.shape, const),
            pl.BlockSpec(wkt.shape, const),
            pl.BlockSpec(wv.shape, const),
            pl.BlockSpec((1, rows, MLA_PAD), lambda b, t: (b, t, 0)),
            pl.BlockSpec((1, rows, MLA_PAD), lambda b, t: (b, t, 0)),
        ],
        out_specs=[
            pl.BlockSpec((1, rows, hp), lambda b, t: (b, t, 0)),
            pl.BlockSpec((1, MLA_HEADS, 1, MLA_PAD, rows), lambda b, t: (b, 0, t, 0, 0)),
            pl.BlockSpec((1, rows, hp), lambda b, t: (b, t, 0)),
            pl.BlockSpec((1, rows, MLA_W), lambda b, t: (b, t, 0)),
        ],
        out_shape=[
            jax.ShapeDtypeStruct((bsz, seq, hp), BF16),
            jax.ShapeDtypeStruct((bsz, MLA_HEADS, n_blk, MLA_PAD, rows), BF16),
            jax.ShapeDtypeStruct((bsz, seq, hp), BF16),
            jax.ShapeDtypeStruct((bsz, seq, MLA_W), BF16),
        ],
        compiler_params=pltpu.CompilerParams(
            dimension_semantics=("arbitrary", "arbitrary"), vmem_limit_bytes=VMEM_LIMIT),
        name="mla_prep",
    )(xb, w_b, qn, kvn, wqa, wqb, wkt, wv, ctab, stab)


def _mla_attn_body(q_ref, kt_ref, v_ref, g_ref, y_ref, *, blk):
    qi = pl.program_id(2)
    r_i = lax.broadcasted_iota(jnp.int32, (blk, blk), 0)
    c_i = lax.broadcasted_iota(jnp.int32, (blk, blk), 1)
    outs = []
    for hh in range(2):
        cols = slice(hh * MLA_PAD, (hh + 1) * MLA_PAD)
        q = q_ref[0, :, cols]

        def step(j, carry, masked):
            m, l, acc = carry
            s = _dot(q, kt_ref[0, hh, j])
            if masked:
                s = jnp.where(r_i >= c_i, s, NEG_BIG)
            m_new = jnp.maximum(m, jnp.max(s, axis=-1, keepdims=True))
            p = jnp.exp(s - m_new)
            alpha = jnp.exp(m - m_new)
            l = alpha * l + jnp.sum(p, axis=-1, keepdims=True)
            start = pl.multiple_of(j * blk, blk)
            v = v_ref[0, pl.ds(start, blk), cols]
            acc = alpha * acc + _dot(p.astype(BF16), v)
            return m_new, l, acc

        init = (jnp.full((blk, 1), NEG_BIG, F32), jnp.zeros((blk, 1), F32),
                jnp.zeros((blk, MLA_PAD), F32))
        carry = lax.fori_loop(0, qi, functools.partial(step, masked=False), init)
        m, l, acc = step(qi, carry, masked=True)
        outs.append((acc * (1.0 / l))[:, :MLA_VDIM])
    o = jnp.concatenate(outs, axis=1)
    y_ref[0] = (o * g_ref[0].astype(F32)).astype(y_ref.dtype)


def _mla_attn_call(q, kt, v, gate, *, blk):
    bsz, seq, hp = q.shape
    n_blk = seq // blk
    body = functools.partial(_mla_attn_body, blk=blk)
    return pl.pallas_call(
        body,
        grid=(bsz, MLA_HEADS // 2, n_blk),
        in_specs=[
            pl.BlockSpec((1, blk, 2 * MLA_PAD), lambda b, h, i: (b, i, h)),
            pl.BlockSpec((1, 2, n_blk, MLA_PAD, blk), lambda b, h, i: (b, h, 0, 0, 0)),
            pl.BlockSpec((1, seq, 2 * MLA_PAD), lambda b, h, i: (b, 0, h)),
            pl.BlockSpec((1, blk, 2 * MLA_VDIM), lambda b, h, i: (b, i, h)),
        ],
        out_specs=pl.BlockSpec((1, blk, 2 * MLA_VDIM), lambda b, h, i: (b, i, h)),
        out_shape=jax.ShapeDtypeStruct((bsz, seq, MLA_W), BF16),
        compiler_params=pltpu.CompilerParams(
            dimension_semantics=("arbitrary", "arbitrary", "arbitrary"),
            vmem_limit_bytes=VMEM_LIMIT),
        name="mla_attn",
    )(q, kt, v, gate)


def _mem_attn_body(x_ref, w_ref, mem_ref, wk_ref, wv_ref, y_ref, k_ref, v_ref):
    @pl.when(pl.program_id(1) == 0)
    def _project_memory():
        mem = mem_ref[0]
        k_ref[...] = _dot(mem, wk_ref[...]).astype(k_ref.dtype)
        v_ref[...] = _dot(mem, wv_ref[...]).astype(v_ref.dtype)

    x = x_ref[0]
    h = _dot(x, w_ref[...])
    scale = 1.0 / math.sqrt(MEM_DIM)
    for hd in range(MEM_HEADS):
        cols = slice(hd * MEM_DIM, (hd + 1) * MEM_DIM)
        q = (h[:, cols] * scale).astype(BF16)
        s = _nt_dot(q, k_ref[:, cols])
        p = jnp.exp(s - jnp.max(s, axis=-1, keepdims=True))
        inv = 1.0 / jnp.sum(p, axis=-1, keepdims=True)
        o = _dot(p.astype(BF16), v_ref[:, cols]) * inv
        gate = h[:, MEM_W + hd * MEM_DIM:MEM_W + (hd + 1) * MEM_DIM]
        y_ref[0, :, cols] = (o * _silu(gate)).astype(y_ref.dtype)


def _mem_attn_call(xb, w_m, memb, wk, wv, *, rows):
    bsz, seq, d = xb.shape
    mem_len = memb.shape[1]
    const = lambda b, t: (0, 0)
    return pl.pallas_call(
        _mem_attn_body,
        grid=(bsz, seq // rows),
        in_specs=[
            pl.BlockSpec((1, rows, d), lambda b, t: (b, t, 0)),
            pl.BlockSpec(w_m.shape, const),
            pl.BlockSpec((1, mem_len, d), lambda b, t: (b, 0, 0)),
            pl.BlockSpec(wk.shape, const),
            pl.BlockSpec(wv.shape, const),
        ],
        out_specs=pl.BlockSpec((1, rows, MEM_W), lambda b, t: (b, t, 0)),
        out_shape=jax.ShapeDtypeStruct((bsz, seq, MEM_W), BF16),
        scratch_shapes=[pltpu.VMEM((mem_len, MEM_W), BF16), pltpu.VMEM((mem_len, MEM_W), BF16)],
        compiler_params=pltpu.CompilerParams(
            dimension_semantics=("arbitrary", "arbitrary"), vmem_limit_bytes=VMEM_LIMIT),
        name="mem_attn",
    )(xb, w_m, memb, wk, wv)


def _out_ln_body(yh_ref, ya_ref, ym_ref, x_ref, w_ref, g_ref, b_ref, o_ref, ob_ref, *, alpha):
    w = w_ref
    y = (_dot(yh_ref[0], w[:HG_W]) + _dot(ya_ref[0], w[HG_W:HG_W + MLA_W])
         + _dot(ym_ref[0], w[HG_W + MLA_W:]))
    z = alpha * x_ref[0] + y
    mu = jnp.mean(z, axis=-1, keepdims=True)
    zc = z - mu
    var = jnp.mean(zc * zc, axis=-1, keepdims=True)
    out = zc * lax.rsqrt(var + LN_EPS) * g_ref[...] + b_ref[...]
    o_ref[0] = out
    ob_ref[0] = out.astype(ob_ref.dtype)


def _out_ln_call(y_hg, y_mla, y_mem, x, w_out, g, b, *, rows, alpha):
    bsz, seq, d = x.shape
    const = lambda b_, t: (0, 0)
    body = functools.partial(_out_ln_body, alpha=alpha)
    row_blk = lambda width: pl.BlockSpec((1, rows, width), lambda b_, t: (b_, t, 0))
    return pl.pallas_call(
        body,
        grid=(bsz, seq // rows),
        in_specs=[row_blk(HG_W), row_blk(MLA_W), row_blk(MEM_W), row_blk(d),
                  pl.BlockSpec(w_out.shape, const),
                  pl.BlockSpec((1, d), const), pl.BlockSpec((1, d), const)],
        out_specs=[row_blk(d), row_blk(d)],
        out_shape=[jax.ShapeDtypeStruct((bsz, seq, d), F32),
                   jax.ShapeDtypeStruct((bsz, seq, d), BF16)],
        compiler_params=pltpu.CompilerParams(
            dimension_semantics=("arbitrary", "arbitrary"), vmem_limit_bytes=VMEM_LIMIT),
        name="out_ln",
    )(y_hg, y_mla, y_mem, x, w_out, g, b)


def _rotate_half_cols(w):
    half = w.shape[-1] // 2
    return jnp.concatenate([-w[..., half:], w[..., :half]], axis=-1)


def _pad_heads(w, heads, width):
    k = w.shape[0]
    w = w.reshape(k, heads, width)
    w = jnp.pad(w, ((0, 0), (0, 0), (0, MLA_PAD - width)))
    return w.reshape(k, heads * MLA_PAD)


def _mla_weights(w_in_l, w_uq, w_uk, w_uv):
    o = 4 * HG_W
    w_cq = w_in_l[:, o:o + MLA_Q_RANK]
    w_ckv = w_in_l[:, o + MLA_Q_RANK:o + MLA_Q_RANK + MLA_KV_RANK]
    o2 = o + MLA_Q_RANK + MLA_KV_RANK
    w_kr = w_in_l[:, o2:o2 + MLA_ROPE]
    w_gate = w_in_l[:, o2 + MLA_ROPE:o2 + MLA_ROPE + MLA_W]
    d = w_in_l.shape[0]
    lane_pad = lambda w: jnp.pad(w, ((0, 0), (MLA_NOPE, MLA_PAD - MLA_QK)))
    w_b = jnp.concatenate([w_cq, w_ckv, lane_pad(w_kr), lane_pad(_rotate_half_cols(w_kr)), w_gate],
                          axis=1).astype(BF16)

    uq = w_uq.reshape(MLA_Q_RANK, MLA_HEADS, MLA_QK)
    uq_rot = jnp.concatenate([jnp.zeros_like(uq[..., :MLA_NOPE]),
                              _rotate_half_cols(uq[..., MLA_NOPE:])], axis=-1)
    wqa = _pad_heads(uq.reshape(MLA_Q_RANK, -1), MLA_HEADS, MLA_QK).astype(BF16)
    wqb = _pad_heads(uq_rot.reshape(MLA_Q_RANK, -1), MLA_HEADS, MLA_QK).astype(BF16)

    uk_t = w_uk.reshape(MLA_KV_RANK, MLA_HEADS, MLA_NOPE).transpose(1, 2, 0)
    top = jnp.concatenate([uk_t, jnp.zeros((MLA_HEADS, MLA_NOPE, MLA_PAD), uk_t.dtype)], axis=-1)
    pick = jnp.eye(MLA_PAD, dtype=uk_t.dtype)[MLA_NOPE:MLA_QK]
    mid = jnp.concatenate([jnp.zeros((MLA_ROPE, MLA_KV_RANK), uk_t.dtype), pick], axis=-1)
    mid = jnp.broadcast_to(mid, (MLA_HEADS,) + mid.shape)
    bot = jnp.zeros((MLA_HEADS, MLA_PAD - MLA_QK, MLA_KV_RANK + MLA_PAD), uk_t.dtype)
    wkt = jnp.concatenate([top, mid, bot], axis=1).reshape(MLA_HEADS * MLA_PAD, -1).astype(BF16)

    wv = _pad_heads(w_uv, MLA_HEADS, MLA_VDIM).astype(BF16)
    return w_b, wqa, wqb, wkt, wv


def _rope_tables(positions):
    inv_freq = 1.0 / (10000.0 ** (jnp.arange(0, MLA_ROPE, 2, dtype=F32) / MLA_ROPE))
    ang = positions.astype(F32)[..., None] * inv_freq
    cos, sin = jnp.cos(ang), jnp.sin(ang)
    shape = ang.shape[:-1]
    ctab = jnp.concatenate([jnp.ones(shape + (MLA_NOPE,), F32), cos, cos,
                            jnp.zeros(shape + (MLA_PAD - MLA_QK,), F32)], axis=-1)
    stab = jnp.concatenate([jnp.zeros(shape + (MLA_NOPE,), F32), sin, sin,
                            jnp.zeros(shape + (MLA_PAD - MLA_QK,), F32)], axis=-1)
    return ctab, stab


def _tile_rows(seq, want):
    rows = min(seq, want)
    assert seq % rows == 0
    return rows


def kernel(x, mem, positions, w_in, hgrn_lb_logits, hgrn_out_norm, mla_q_norm, mla_kv_norm,
           w_mla_uq, w_mla_uk, w_mla_uv, w_mem_k, w_mem_v, w_out, ln_g, ln_b):
    depth = w_in.shape[0]
    seq = x.shape[1]
    alpha = (2 * depth) ** 0.25
    ctab, stab = _rope_tables(positions)
    lb_soft = jax.nn.softmax(hgrn_lb_logits.astype(F32), axis=0)
    lower_bounds = jnp.cumsum(lb_soft, axis=0) - lb_soft[0]
    memb = mem.astype(BF16)
    xb = x.astype(BF16)
    o_mem = 4 * HG_W + MLA_Q_RANK + MLA_KV_RANK + MLA_ROPE + MLA_W

    for l in range(depth):
        w_l = w_in[l]
        w_hg = w_l[:, :4 * HG_W].astype(BF16)
        w_b, wqa, wqb, wkt, wv = _mla_weights(w_l, w_mla_uq[l], w_mla_uk[l], w_mla_uv[l])
        w_m = w_l[:, o_mem:].astype(BF16)

        y_hg = _hgrn_call(xb, w_hg, lower_bounds[l][None, :], hgrn_out_norm[l][None, :],
                          rows=_tile_rows(seq, 256))
        blk = _tile_rows(seq, 512)
        q, kt, v, gate = _mla_prep_call(xb, w_b, mla_q_norm[l][None, :], mla_kv_norm[l][None, :],
                                        wqa, wqb, wkt, wv, ctab, stab, rows=blk)
        y_mla = _mla_attn_call(q, kt, v, gate, blk=blk)
        y_mem = _mem_attn_call(xb, w_m, memb, w_mem_k[l].astype(BF16), w_mem_v[l].astype(BF16),
                               rows=_tile_rows(seq, 512))
        x, xb = _out_ln_call(y_hg, y_mla, y_mem, x, w_out[l].astype(BF16),
                             ln_g[l][None, :], ln_b[l][None, :],
                             rows=_tile_rows(seq, 512), alpha=alpha)
    return x
```

```python
import functools
import math

import jax
import jax.numpy as jnp
from jax import lax
from jax.experimental import pallas as pl
from jax.experimental.pallas import tpu as pltpu

F32 = jnp.float32
BF16 = jnp.bfloat16

LANES = 128

HG_HEADS = 8
HG_DIM = 128
HG_W = HG_HEADS * HG_DIM
HG_CHUNK = 64
HG_HALF = HG_CHUNK // 2
HG_QUARTER = HG_CHUNK // 4
HG_SAFE_DECAY = 40.0

MLA_HEADS = 8
MLA_NOPE = 64
MLA_ROPE = 32
MLA_QK = MLA_NOPE + MLA_ROPE
MLA_VDIM = 64
MLA_Q_RANK = 256
MLA_KV_RANK = 128
MLA_PAD = LANES
MLA_W = MLA_HEADS * MLA_VDIM
ATTN_WIDE_SUBS = 4

MEM_HEADS = 4
MEM_DIM = 128
MEM_W = MEM_HEADS * MEM_DIM

RMS_EPS = 1e-6
LN_EPS = 1e-5
NEG_BIG = -1e30
LOG2_E = math.log2(math.e)

VMEM_LIMIT = 56 * 1024 * 1024


def _nt_dot(a, b):
    return lax.dot_general(a, b, (((1,), (1,)), ((), ())), preferred_element_type=F32)


def _tn_dot(a, b):
    return lax.dot_general(a, b, (((0,), (0,)), ((), ())), preferred_element_type=F32)


def _dot(a, b):
    return jnp.dot(a, b, preferred_element_type=F32)


def _silu(x):
    return x * (1.0 / (1.0 + jnp.exp(-x)))


def _hgrn_body(x_ref, w_ref, lb_ref, g_ref, y_ref,
               h_ref, q_ref, k_ref, b_ref, a_ref, st_ref, *, rows):
    n_chunk = rows // HG_CHUNK

    @pl.when(pl.program_id(1) == 0)
    def _reset_state():
        st_ref[...] = jnp.zeros_like(st_ref)

    x = x_ref[0]
    for g in range(4):
        cols = slice(g * HG_W, (g + 1) * HG_W)
        h_ref[:, cols] = _dot(x, w_ref[:, cols])

    for h in range(HG_HEADS):
        cols = slice(HG_W + h * HG_DIM, HG_W + (h + 1) * HG_DIM)
        z = h_ref[:, cols]
        lb = lb_ref[:, h * HG_DIM:(h + 1) * HG_DIM]
        e = jnp.exp(-z)
        u = 1.0 + e
        h_ref[:, cols] = jnp.log(1.0 + lb * e) - jnp.log(u)
        k_ref[h] = (1.0 - lb) * (1.0 - 1.0 / u)
        q_ref[h] = h_ref[:, h * HG_DIM:(h + 1) * HG_DIM]

    r_i = lax.broadcasted_iota(jnp.int32, (HG_CHUNK, HG_CHUNK), 0)
    c_i = lax.broadcasted_iota(jnp.int32, (HG_CHUNK, HG_CHUNK), 1)
    causal = r_i >= c_i
    tri = jnp.where(causal, 1.0, 0.0).astype(BF16)
    worst = jnp.zeros((1, HG_W), F32)
    for j in range(n_chunk):
        rws = slice(j * HG_CHUNK, (j + 1) * HG_CHUNK)
        lf = h_ref[rws, HG_W:2 * HG_W]
        p0 = lf.astype(BF16)
        r0 = lf - p0.astype(F32)
        p1 = r0.astype(BF16)
        p2 = (r0 - p1.astype(F32)).astype(BF16)
        b = _dot(tri, p0) + _dot(tri, p1) + _dot(tri, p2)
        for h in range(HG_HEADS):
            b_ref[h, rws, :] = b[:, h * HG_DIM:(h + 1) * HG_DIM]
        q1 = b[HG_QUARTER - 1:HG_QUARTER]
        q2 = b[2 * HG_QUARTER - 1:2 * HG_QUARTER]
        q3 = b[3 * HG_QUARTER - 1:3 * HG_QUARTER]
        q4 = b[HG_CHUNK - 1:HG_CHUNK]
        worst = jnp.maximum(worst, jnp.maximum(jnp.maximum(-q1, q1 - q2),
                                               jnp.maximum(q2 - q3, q3 - q4)))
    unsafe = jnp.max(worst) > HG_SAFE_DECAY

    zeros_half = jnp.zeros((HG_HALF, HG_DIM), BF16)
    for j in range(n_chunk):
        rws = slice(j * HG_CHUNK, (j + 1) * HG_CHUNK)
        for h in range(HG_HEADS):
            bc = b_ref[h, rws, :]
            qc = q_ref[h, rws, :]
            kc = k_ref[h, rws, :]
            b_lo, b_hi = bc[:HG_HALF], bc[HG_HALF:]
            m0 = bc[HG_QUARTER - 1:HG_QUARTER]
            m1 = bc[3 * HG_QUARTER - 1:3 * HG_QUARTER]
            mid = bc[HG_HALF - 1:HG_HALF]
            qd0 = (qc[:HG_HALF] * jnp.exp(b_lo - m0)).astype(BF16)
            kd0 = (kc[:HG_HALF] * jnp.exp(m0 - b_lo)).astype(BF16)
            qd1 = (qc[HG_HALF:] * jnp.exp(b_hi - m1)).astype(BF16)
            kd1 = (kc[HG_HALF:] * jnp.exp(m1 - b_hi)).astype(BF16)
            qo = (qc[HG_HALF:] * jnp.exp(b_hi - mid)).astype(BF16)
            ko = (kc[:HG_HALF] * jnp.exp(mid - b_lo)).astype(BF16)
            q_cat = jnp.concatenate([
                jnp.concatenate([qd0, zeros_half], axis=0),
                jnp.concatenate([zeros_half, qd1], axis=0),
                jnp.concatenate([zeros_half, qo], axis=0)], axis=1)
            k_cat = jnp.concatenate([
                jnp.concatenate([kd0, zeros_half], axis=0),
                jnp.concatenate([zeros_half, kd1], axis=0),
                jnp.concatenate([ko, zeros_half], axis=0)], axis=1)
            a = _nt_dot(q_cat, k_cat)
            a_ref[j * HG_HEADS + h] = jnp.where(causal, a, 0.0)

    @pl.when(unsafe)
    def _direct_scores():
        def per_chunk_head(idx, carry):
            j = idx // HG_HEADS
            h = idx - j * HG_HEADS
            base = pl.multiple_of(j * HG_CHUNK, HG_CHUNK)
            bc = b_ref[h, pl.ds(base, HG_CHUNK), :]
            qc = q_ref[h, pl.ds(base, HG_CHUNK), :]

            def per_key(s, a):
                b_s = b_ref[h, pl.ds(base + s, 1), :]
                k_s = k_ref[h, pl.ds(base + s, 1), :]
                prod = qc * jnp.exp(jnp.minimum(bc - b_s, 0.0)) * k_s
                col = jnp.sum(prod, axis=-1, keepdims=True)
                return jnp.where((c_i == s) & (r_i >= s), col, a)

            a_ref[idx] = lax.fori_loop(0, HG_CHUNK, per_key,
                                       jnp.zeros((HG_CHUNK, HG_CHUNK), F32))
            return carry

        lax.fori_loop(0, n_chunk * HG_HEADS, per_chunk_head, 0)

    gain = g_ref[...]
    for j in range(n_chunk):
        rws = slice(j * HG_CHUNK, (j + 1) * HG_CHUNK)
        for h in range(HG_HEADS):
            hc = slice(h * HG_DIM, (h + 1) * HG_DIM)
            bc = b_ref[h, rws, :]
            qc = q_ref[h, rws, :]
            kc = k_ref[h, rws, :]
            vc = h_ref[rws, 2 * HG_W + h * HG_DIM:2 * HG_W + (h + 1) * HG_DIM].astype(BF16)
            gate = h_ref[rws, 3 * HG_W + h * HG_DIM:3 * HG_W + (h + 1) * HG_DIM]
            b_last = bc[HG_CHUNK - 1:HG_CHUNK]
            st = st_ref[h]
            q_hat = (qc * jnp.exp(bc)).astype(BF16)
            k_dec = (kc * jnp.exp(b_last - bc)).astype(BF16)
            o = _dot(a_ref[j * HG_HEADS + h].astype(BF16), vc) + _nt_dot(q_hat, st.astype(BF16))
            st_ref[h] = st * jnp.exp(b_last) + _tn_dot(vc, k_dec)
            o = o * lax.rsqrt(jnp.mean(o * o, axis=-1, keepdims=True) + RMS_EPS)
            y_ref[0, rws, hc] = (o * gain * _silu(gate)).astype(y_ref.dtype)


def _hgrn_call(xb, w_hg, lb, gain, *, rows):
    bsz, seq, d = xb.shape
    body = functools.partial(_hgrn_body, rows=rows)
    n_chunk = rows // HG_CHUNK
    return pl.pallas_call(
        body,
        grid=(bsz, seq // rows),
        in_specs=[
            pl.BlockSpec((1, rows, d), lambda b, t: (b, t, 0)),
            pl.BlockSpec((d, 4 * HG_W), lambda b, t: (0, 0)),
            pl.BlockSpec((1, HG_W), lambda b, t: (0, 0)),
            pl.BlockSpec((1, HG_DIM), lambda b, t: (0, 0)),
        ],
        out_specs=pl.BlockSpec((1, rows, HG_W), lambda b, t: (b, t, 0)),
        out_shape=jax.ShapeDtypeStruct((bsz, seq, HG_W), BF16),
        scratch_shapes=[
            pltpu.VMEM((rows, 4 * HG_W), F32),
            pltpu.VMEM((HG_HEADS, rows, HG_DIM), F32),
            pltpu.VMEM((HG_HEADS, rows, HG_DIM), F32),
            pltpu.VMEM((HG_HEADS, rows, HG_DIM), F32),
            pltpu.VMEM((n_chunk * HG_HEADS, HG_CHUNK, HG_CHUNK), F32),
            pltpu.VMEM((HG_HEADS, HG_DIM, HG_DIM), F32),
        ],
        compiler_params=pltpu.CompilerParams(
            dimension_semantics=("arbitrary", "arbitrary"), vmem_limit_bytes=VMEM_LIMIT),
        name="hgrn",
    )(xb, w_hg, lb, gain)


def _mla_prep_body(x_ref, w_ref, qn_ref, kvn_ref, wqa_ref, wqb_ref, wk_ref, wv_ref,
                   ct_ref, st_ref, ctt_ref, stt_ref, qt_ref, k_ref, vt_ref, g_ref):
    x = x_ref[0]
    h = _dot(x, w_ref[...])
    cq = h[:, :MLA_Q_RANK]
    ckv = h[:, MLA_Q_RANK:MLA_Q_RANK + MLA_KV_RANK]
    o = MLA_Q_RANK + MLA_KV_RANK
    kr = h[:, o:o + MLA_PAD]
    kr_rot = h[:, o + MLA_PAD:o + 2 * MLA_PAD]
    gate = h[:, o + 2 * MLA_PAD:]

    cqn = (cq * lax.rsqrt(jnp.mean(cq * cq, axis=-1, keepdims=True) + RMS_EPS)
           * qn_ref[...]).astype(BF16)
    c = (ckv * lax.rsqrt(jnp.mean(ckv * ckv, axis=-1, keepdims=True) + RMS_EPS)
         * kvn_ref[...]).astype(BF16)

    scale = LOG2_E / math.sqrt(MLA_QK)
    qa_t = _nt_dot(wqa_ref[...], cqn)
    qb_t = _nt_dot(wqb_ref[...], cqn)
    cq_tab = ctt_ref[0] * scale
    sq_tab = stt_ref[0] * scale
    for hd in range(MLA_HEADS):
        rws = slice(hd * MLA_PAD, (hd + 1) * MLA_PAD)
        qt_ref[0, rws, :] = (qa_t[rws] * cq_tab + qb_t[rws] * sq_tab).astype(qt_ref.dtype)

    kr_roped = (kr * ct_ref[0] + kr_rot * st_ref[0]).astype(BF16)
    ck = jnp.concatenate([c, kr_roped], axis=1)
    k_ref[0] = _dot(ck, wk_ref[...]).astype(k_ref.dtype)

    vt = _nt_dot(wv_ref[...], c).astype(vt_ref.dtype)
    bk = vt_ref.shape[-1]
    for hd in range(MLA_HEADS):
        for sub in range(vt_ref.shape[2]):
            vt_ref[0, hd, sub] = vt[hd * MLA_VDIM:(hd + 1) * MLA_VDIM, sub * bk:(sub + 1) * bk]
    g_ref[0] = _silu(gate).astype(g_ref.dtype)


def _mla_prep_call(xb, w_b, qn, kvn, wqa, wqb, wk, wv, ctab, stab, ctab_t, stab_t, *, rows, bk):
    bsz, seq, d = xb.shape
    n_blk = seq // rows
    hp = MLA_HEADS * MLA_PAD
    const = lambda b, t: (0, 0)
    return pl.pallas_call(
        _mla_prep_body,
        grid=(bsz, n_blk),
        in_specs=[
            pl.BlockSpec((1, rows, d), lambda b, t: (b, t, 0)),
            pl.BlockSpec(w_b.shape, const),
            pl.BlockSpec((1, MLA_Q_RANK), const),
            pl.BlockSpec((1, MLA_KV_RANK), const),
            pl.BlockSpec(wqa.shape, const),
            pl.BlockSpec(wqb.shape, const),
            pl.BlockSpec(wk.shape, const),
            pl.BlockSpec(wv.shape, const),
            pl.BlockSpec((1, rows, MLA_PAD), lambda b, t: (b, t, 0)),
            pl.BlockSpec((1, rows, MLA_PAD), lambda b, t: (b, t, 0)),
            pl.BlockSpec((1, MLA_PAD, rows), lambda b, t: (b, 0, t)),
            pl.BlockSpec((1, MLA_PAD, rows), lambda b, t: (b, 0, t)),
        ],
        out_specs=[
            pl.BlockSpec((1, hp, rows), lambda b, t: (b, 0, t)),
            pl.BlockSpec((1, rows, hp), lambda b, t: (b, t, 0)),
            pl.BlockSpec((1, MLA_HEADS, rows // bk, MLA_VDIM, bk), lambda b, t: (b, 0, t, 0, 0)),
            pl.BlockSpec((1, rows, MLA_W), lambda b, t: (b, t, 0)),
        ],
        out_shape=[
            jax.ShapeDtypeStruct((bsz, hp, seq), BF16),
            jax.ShapeDtypeStruct((bsz, seq, hp), BF16),
            jax.ShapeDtypeStruct((bsz, MLA_HEADS, seq // bk, MLA_VDIM, bk), BF16),
            jax.ShapeDtypeStruct((bsz, seq, MLA_W), BF16),
        ],
        compiler_params=pltpu.CompilerParams(
            dimension_semantics=("arbitrary", "arbitrary"), vmem_limit_bytes=VMEM_LIMIT),
        name="mla_prep",
    )(xb, w_b, qn, kvn, wqa, wqb, wk, wv, ctab, stab, ctab_t, stab_t)


def _mla_attn_body(qt_ref, k_ref, vt_ref, g_ref, y_ref, *, bq, bk):
    qi = pl.program_id(2)
    n_diag = bq // bk
    assert ATTN_WIDE_SUBS % n_diag == 0
    key_i = lax.broadcasted_iota(jnp.int32, (bk, bq), 0)
    qry_i = lax.broadcasted_iota(jnp.int32, (bk, bq), 1)
    q_t = [qt_ref[0, hh * MLA_PAD:(hh + 1) * MLA_PAD, :] for hh in range(2)]

    def step(first_blk, carry, n_sub, masked):
        carry = list(carry)
        scores = []
        for sub in range(n_sub):
            start = pl.multiple_of((first_blk + sub) * bk, bk)
            scores.append([_dot(k_ref[0, pl.ds(start, bk), hh * MLA_PAD:(hh + 1) * MLA_PAD],
                                q_t[hh]) for hh in range(2)])
        for sub in range(n_sub):
            for hh in range(2):
                m, l, acc = carry[hh]
                s = scores[sub][hh]
                if masked:
                    s = jnp.where(key_i + sub * bk <= qry_i, s, NEG_BIG)
                m_new = jnp.maximum(m, jnp.max(s, axis=0, keepdims=True))
                p = jnp.exp2(s - m_new)
                alpha = jnp.exp2(m - m_new)
                l = alpha * l + jnp.sum(p, axis=0, keepdims=True)
                acc = alpha * acc + _dot(vt_ref[0, hh, first_blk + sub], p.astype(BF16))
                carry[hh] = (m_new, l, acc)
        return tuple(carry)

    init = tuple((jnp.full((1, bq), NEG_BIG, F32), jnp.zeros((1, bq), F32),
                  jnp.zeros((MLA_VDIM, bq), F32)) for _ in range(2))
    n_full = qi * n_diag
    n_wide = n_full // ATTN_WIDE_SUBS
    carry = lax.fori_loop(
        0, n_wide, lambda t, c: step(t * ATTN_WIDE_SUBS, c, ATTN_WIDE_SUBS, False), init)
    carry = lax.fori_loop(
        n_wide * (ATTN_WIDE_SUBS // n_diag), qi, lambda t, c: step(t * n_diag, c, n_diag, False),
        carry)
    carry = step(n_full, carry, n_diag, True)
    o_t = jnp.concatenate([acc * (1.0 / l) for (_, l, acc) in carry], axis=0)
    y_ref[0] = (o_t.T * g_ref[0].astype(F32)).astype(y_ref.dtype)


def _mla_attn_call(qt, k, vt, gate, *, bq):
    bsz, seq, hp = k.shape
    bk = vt.shape[-1]
    body = functools.partial(_mla_attn_body, bq=bq, bk=bk)
    return pl.pallas_call(
        body,
        grid=(bsz, MLA_HEADS // 2, seq // bq),
        in_specs=[
            pl.BlockSpec((1, 2 * MLA_PAD, bq), lambda b, h, i: (b, h, i)),
            pl.BlockSpec((1, seq, 2 * MLA_PAD), lambda b, h, i: (b, 0, h)),
            pl.BlockSpec((1, 2, seq // bk, MLA_VDIM, bk), lambda b, h, i: (b, h, 0, 0, 0)),
            pl.BlockSpec((1, bq, 2 * MLA_VDIM), lambda b, h, i: (b, i, h)),
        ],
        out_specs=pl.BlockSpec((1, bq, 2 * MLA_VDIM), lambda b, h, i: (b, i, h)),
        out_shape=jax.ShapeDtypeStruct((bsz, seq, MLA_W), BF16),
        compiler_params=pltpu.CompilerParams(
            dimension_semantics=("arbitrary", "arbitrary", "arbitrary"),
            vmem_limit_bytes=VMEM_LIMIT),
        name="mla_attn",
    )(qt, k, vt, gate)


def _mem_attn_body(x_ref, w_ref, mem_ref, wk_ref, wv_ref, y_ref, k_ref, v_ref):
    @pl.when(pl.program_id(1) == 0)
    def _project_memory():
        mem = mem_ref[0]
        k_ref[...] = _dot(mem, wk_ref[...]).astype(k_ref.dtype)
        v_ref[...] = _dot(mem, wv_ref[...]).astype(v_ref.dtype)

    x = x_ref[0]
    h = _dot(x, w_ref[...])
    scale = 1.0 / math.sqrt(MEM_DIM)
    for hd in range(MEM_HEADS):
        cols = slice(hd * MEM_DIM, (hd + 1) * MEM_DIM)
        q = (h[:, cols] * scale).astype(BF16)
        s = _nt_dot(q, k_ref[:, cols])
        p = jnp.exp(s - jnp.max(s, axis=-1, keepdims=True))
        inv = 1.0 / jnp.sum(p, axis=-1, keepdims=True)
        o = _dot(p.astype(BF16), v_ref[:, cols]) * inv
        gate = h[:, MEM_W + hd * MEM_DIM:MEM_W + (hd + 1) * MEM_DIM]
        y_ref[0, :, cols] = (o * _silu(gate)).astype(y_ref.dtype)


def _mem_attn_call(xb, w_m, memb, wk, wv, *, rows):
    bsz, seq, d = xb.shape
    mem_len = memb.shape[1]
    const = lambda b, t: (0, 0)
    return pl.pallas_call(
        _mem_attn_body,
        grid=(bsz, seq // rows),
        in_specs=[
            pl.BlockSpec((1, rows, d), lambda b, t: (b, t, 0)),
            pl.BlockSpec(w_m.shape, const),
            pl.BlockSpec((1, mem_len, d), lambda b, t: (b, 0, 0)),
            pl.BlockSpec(wk.shape, const),
            pl.BlockSpec(wv.shape, const),
        ],
        out_specs=pl.BlockSpec((1, rows, MEM_W), lambda b, t: (b, t, 0)),
        out_shape=jax.ShapeDtypeStruct((bsz, seq, MEM_W), BF16),
        scratch_shapes=[pltpu.VMEM((mem_len, MEM_W), BF16), pltpu.VMEM((mem_len, MEM_W), BF16)],
        compiler_params=pltpu.CompilerParams(
            dimension_semantics=("arbitrary", "arbitrary"), vmem_limit_bytes=VMEM_LIMIT),
        name="mem_attn",
    )(xb, w_m, memb, wk, wv)


def _out_ln_body(yh_ref, ya_ref, ym_ref, x_ref, w_ref, g_ref, b_ref, o_ref, ob_ref, *, alpha):
    w = w_ref
    y = (_dot(yh_ref[0], w[:HG_W]) + _dot(ya_ref[0], w[HG_W:HG_W + MLA_W])
         + _dot(ym_ref[0], w[HG_W + MLA_W:]))
    z = alpha * x_ref[0] + y
    mu = jnp.mean(z, axis=-1, keepdims=True)
    zc = z - mu
    var = jnp.mean(zc * zc, axis=-1, keepdims=True)
    out = zc * lax.rsqrt(var + LN_EPS) * g_ref[...] + b_ref[...]
    o_ref[0] = out
    ob_ref[0] = out.astype(ob_ref.dtype)


def _out_ln_call(y_hg, y_mla, y_mem, x, w_out, g, b, *, rows, alpha):
    bsz, seq, d = x.shape
    const = lambda b_, t: (0, 0)
    body = functools.partial(_out_ln_body, alpha=alpha)
    row_blk = lambda width: pl.BlockSpec((1, rows, width), lambda b_, t: (b_, t, 0))
    return pl.pallas_call(
        body,
        grid=(bsz, seq // rows),
        in_specs=[row_blk(HG_W), row_blk(MLA_W), row_blk(MEM_W), row_blk(d),
                  pl.BlockSpec(w_out.shape, const),
                  pl.BlockSpec((1, d), const), pl.BlockSpec((1, d), const)],
        out_specs=[row_blk(d), row_blk(d)],
        out_shape=[jax.ShapeDtypeStruct((bsz, seq, d), F32),
                   jax.ShapeDtypeStruct((bsz, seq, d), BF16)],
        compiler_params=pltpu.CompilerParams(
            dimension_semantics=("arbitrary", "arbitrary"), vmem_limit_bytes=VMEM_LIMIT),
        name="out_ln",
    )(y_hg, y_mla, y_mem, x, w_out, g, b)


def _rotate_half_cols(w):
    half = w.shape[-1] // 2
    return jnp.concatenate([-w[..., half:], w[..., :half]], axis=-1)


def _pad_heads(w, heads, width):
    k = w.shape[0]
    w = w.reshape(k, heads, width)
    w = jnp.pad(w, ((0, 0), (0, 0), (0, MLA_PAD - width)))
    return w.reshape(k, heads * MLA_PAD)


def _mla_weights(w_in_l, w_uq, w_uk, w_uv):
    o = 4 * HG_W
    w_cq = w_in_l[:, o:o + MLA_Q_RANK]
    w_ckv = w_in_l[:, o + MLA_Q_RANK:o + MLA_Q_RANK + MLA_KV_RANK]
    o2 = o + MLA_Q_RANK + MLA_KV_RANK
    w_kr = w_in_l[:, o2:o2 + MLA_ROPE]
    w_gate = w_in_l[:, o2 + MLA_ROPE:o2 + MLA_ROPE + MLA_W]
    d = w_in_l.shape[0]
    lane_pad = lambda w: jnp.pad(w, ((0, 0), (MLA_NOPE, MLA_PAD - MLA_QK)))
    w_b = jnp.concatenate([w_cq, w_ckv, lane_pad(w_kr), lane_pad(_rotate_half_cols(w_kr)), w_gate],
                          axis=1).astype(BF16)

    uq = w_uq.reshape(MLA_Q_RANK, MLA_HEADS, MLA_QK)
    uq_rot = jnp.concatenate([jnp.zeros_like(uq[..., :MLA_NOPE]),
                              _rotate_half_cols(uq[..., MLA_NOPE:])], axis=-1)
    wqa = _pad_heads(uq.reshape(MLA_Q_RANK, -1), MLA_HEADS, MLA_QK).T.astype(BF16)
    wqb = _pad_heads(uq_rot.reshape(MLA_Q_RANK, -1), MLA_HEADS, MLA_QK).T.astype(BF16)

    uk = w_uk.reshape(MLA_KV_RANK, MLA_HEADS, MLA_NOPE)
    top = jnp.pad(uk, ((0, 0), (0, 0), (0, MLA_PAD - MLA_NOPE)))
    pick = jnp.eye(MLA_PAD, dtype=uk.dtype)
    pick = pick * ((jnp.arange(MLA_PAD) >= MLA_NOPE) & (jnp.arange(MLA_PAD) < MLA_QK))
    bot = jnp.broadcast_to(pick[:, None, :], (MLA_PAD, MLA_HEADS, MLA_PAD))
    wk = jnp.concatenate([top, bot], axis=0).reshape(MLA_KV_RANK + MLA_PAD, -1).astype(BF16)

    wv = w_uv.T.astype(BF16)
    return w_b, wqa, wqb, wk, wv


def _rope_tables(positions):
    inv_freq = 1.0 / (10000.0 ** (jnp.arange(0, MLA_ROPE, 2, dtype=F32) / MLA_ROPE))
    ang = positions.astype(F32)[..., None] * inv_freq
    cos, sin = jnp.cos(ang), jnp.sin(ang)
    shape = ang.shape[:-1]
    ctab = jnp.concatenate([jnp.ones(shape + (MLA_NOPE,), F32), cos, cos,
                            jnp.zeros(shape + (MLA_PAD - MLA_QK,), F32)], axis=-1)
    stab = jnp.concatenate([jnp.zeros(shape + (MLA_NOPE,), F32), sin, sin,
                            jnp.zeros(shape + (MLA_PAD - MLA_QK,), F32)], axis=-1)
    return ctab, stab


def _tile_rows(seq, want):
    rows = min(seq, want)
    assert seq % rows == 0
    return rows


def kernel(x, mem, positions, w_in, hgrn_lb_logits, hgrn_out_norm, mla_q_norm, mla_kv_norm,
           w_mla_uq, w_mla_uk, w_mla_uv, w_mem_k, w_mem_v, w_out, ln_g, ln_b):
    depth = w_in.shape[0]
    seq = x.shape[1]
    alpha = (2 * depth) ** 0.25
    ctab, stab = _rope_tables(positions)
    ctab_t, stab_t = jnp.swapaxes(ctab, 1, 2), jnp.swapaxes(stab, 1, 2)
    lb_soft = jax.nn.softmax(hgrn_lb_logits.astype(F32), axis=0)
    lower_bounds = jnp.cumsum(lb_soft, axis=0) - lb_soft[0]
    memb = mem.astype(BF16)
    xb = x.astype(BF16)
    o_mem = 4 * HG_W + MLA_Q_RANK + MLA_KV_RANK + MLA_ROPE + MLA_W

    for l in range(depth):
        w_l = w_in[l]
        w_hg = w_l[:, :4 * HG_W].astype(BF16)
        w_b, wqa, wqb, wk, wv = _mla_weights(w_l, w_mla_uq[l], w_mla_uk[l], w_mla_uv[l])
        w_m = w_l[:, o_mem:].astype(BF16)

        y_hg = _hgrn_call(xb, w_hg, lower_bounds[l][None, :], hgrn_out_norm[l][None, :],
                          rows=_tile_rows(seq, 256))
        blk = _tile_rows(seq, 512)
        qt, k, vt, gate = _mla_prep_call(xb, w_b, mla_q_norm[l][None, :], mla_kv_norm[l][None, :],
                                         wqa, wqb, wk, wv, ctab, stab, ctab_t, stab_t,
                                         rows=blk, bk=blk // 2)
        y_mla = _mla_attn_call(qt, k, vt, gate, bq=blk)
        y_mem = _mem_attn_call(xb, w_m, memb, w_mem_k[l].astype(BF16), w_mem_v[l].astype(BF16),
                               rows=_tile_rows(seq, 512))
        x, xb = _out_ln_call(y_hg, y_mla, y_mem, x, w_out[l].astype(BF16),
                             ln_g[l][None, :], ln_b[l][None, :],
                             rows=_tile_rows(seq, 512), alpha=alpha)
    return x
```

```python
import functools
import math

import jax
import jax.numpy as jnp
from jax import lax
from jax.experimental import pallas as pl
from jax.experimental.pallas import tpu as pltpu

F32 = jnp.float32
BF16 = jnp.bfloat16

LANES = 128

HG_HEADS = 8
HG_DIM = 128
HG_W = HG_HEADS * HG_DIM
HG_CHUNK = 64
HG_HALF = HG_CHUNK // 2
HG_SAFE_DECAY = 60.0

MLA_HEADS = 8
MLA_NOPE = 64
MLA_ROPE = 32
MLA_QK = MLA_NOPE + MLA_ROPE
MLA_VDIM = 64
MLA_Q_RANK = 256
MLA_KV_RANK = 128
MLA_PAD = LANES
MLA_W = MLA_HEADS * MLA_VDIM
MLA_VROWS = MLA_VDIM + 16
ATTN_WIDE_SUBS = 4

MEM_HEADS = 4
MEM_DIM = 128
MEM_W = MEM_HEADS * MEM_DIM

RMS_EPS = 1e-6
LN_EPS = 1e-5
NEG_BIG = -1e30
LOG2_E = math.log2(math.e)

VMEM_LIMIT = 56 * 1024 * 1024


def _nt_dot(a, b):
    return lax.dot_general(a, b, (((1,), (1,)), ((), ())), preferred_element_type=F32)


def _dot(a, b):
    return jnp.dot(a, b, preferred_element_type=F32)


def _silu(x):
    return x * (1.0 / (1.0 + jnp.exp(-x)))


def _hgrn_body(x_ref, xn_ref, w_ref, lb_ref, g_ref, y_ref, *scratch, rows):
    half = rows // 2
    n_chunk = half // HG_CHUNK
    sets = (scratch[0:5], scratch[5:10])
    st_ref = scratch[10]

    @pl.when(pl.program_id(1) == 0)
    def _reset_state():
        st_ref[...] = jnp.zeros_like(st_ref)

    r_i = lax.broadcasted_iota(jnp.int32, (HG_CHUNK, HG_DIM), 0)
    c_i = lax.broadcasted_iota(jnp.int32, (HG_CHUNK, HG_DIM), 1)
    causal = r_i >= c_i
    zeros_chunk = jnp.zeros((HG_CHUNK, HG_DIM), BF16)
    zeros_v = jnp.zeros((HG_CHUNK, HG_DIM), F32)
    gain = g_ref[...]

    piece = 2 * HG_W // n_chunk

    def project(hf, c0, c1, nxt=False):
        x = xn_ref[0] if nxt else x_ref[0, hf * half:(hf + 1) * half, :]
        sets[hf][0][:, c0:c1] = _dot(x, w_ref[:, c0:c1])

    def gates(hf):
        h_ref, q_ref, k_ref = sets[hf][0:3]
        for h in range(HG_HEADS):
            cols = slice(HG_W + h * HG_DIM, HG_W + (h + 1) * HG_DIM)
            z = h_ref[:, cols]
            lb = lb_ref[:, h * HG_DIM:(h + 1) * HG_DIM]
            e = jnp.exp(-z)
            u = 1.0 + e
            h_ref[:, cols] = jnp.log(1.0 + lb * e) - jnp.log(u)
            k_ref[h] = (1.0 - lb) * (1.0 - 1.0 / u)
            q_ref[h] = h_ref[:, h * HG_DIM:(h + 1) * HG_DIM]

    def cumulative_decay(hf):
        h_ref, b_ref = sets[hf][0], sets[hf][3]
        row = lax.broadcasted_iota(jnp.int32, (8, HG_W), 0)
        worst = jnp.zeros((1, HG_W), F32)
        for j in range(n_chunk):
            total = jnp.zeros((1, HG_W), F32)
            marks = []
            for t in range(HG_CHUNK // 8):
                rws = slice(j * HG_CHUNK + 8 * t, j * HG_CHUNK + 8 * (t + 1))
                acc = h_ref[rws, HG_W:2 * HG_W]
                for shift in (1, 2, 4):
                    acc = acc + jnp.where(row >= shift, pltpu.roll(acc, shift, axis=0), 0.0)
                acc = acc + total
                total = acc[7:8]
                for h in range(HG_HEADS):
                    b_ref[h, rws, :] = acc[:, h * HG_DIM:(h + 1) * HG_DIM]
                if 8 * (t + 1) in (HG_HALF, HG_CHUNK):
                    marks.append(total)
            worst = jnp.maximum(worst, jnp.maximum(-marks[0], marks[0] - marks[1]))
        return jnp.max(worst) > HG_SAFE_DECAY

    def scores(hf, j):
        _, q_ref, k_ref, b_ref, a_ref = sets[hf]
        rws = slice(j * HG_CHUNK, (j + 1) * HG_CHUNK)
        for h in range(HG_HEADS):
            bc = b_ref[h, rws, :]
            rel = bc - bc[HG_HALF - 1:HG_HALF]
            q_mid = (q_ref[h, rws, :] * jnp.exp(rel)).astype(BF16)
            k_mid = (k_ref[h, rws, :] * jnp.exp(-rel)).astype(BF16)
            k_pad = jnp.concatenate([k_mid, zeros_chunk], axis=0)
            a_ref[j * HG_HEADS + h] = jnp.where(causal, _nt_dot(q_mid, k_pad), 0.0)

    def direct_scores(hf):
        _, q_ref, k_ref, b_ref, a_ref = sets[hf]

        def per_chunk_head(idx, carry):
            j = idx // HG_HEADS
            h = idx - j * HG_HEADS
            base = pl.multiple_of(j * HG_CHUNK, HG_CHUNK)
            bc = b_ref[h, pl.ds(base, HG_CHUNK), :]
            qc = q_ref[h, pl.ds(base, HG_CHUNK), :]

            def per_key(s, a):
                b_s = b_ref[h, pl.ds(base + s, 1), :]
                k_s = k_ref[h, pl.ds(base + s, 1), :]
                prod = qc * jnp.exp(jnp.minimum(bc - b_s, 0.0)) * k_s
                col = jnp.sum(prod, axis=-1, keepdims=True)
                return jnp.where((c_i == s) & (r_i >= s), col, a)

            a_ref[idx] = lax.fori_loop(0, HG_CHUNK, per_key,
                                       jnp.zeros((HG_CHUNK, HG_DIM), F32))
            return carry

        lax.fori_loop(0, n_chunk * HG_HEADS, per_chunk_head, 0)

    def readout(hf, j):
        h_ref, q_ref, k_ref, b_ref, a_ref = sets[hf]
        rws = slice(j * HG_CHUNK, (j + 1) * HG_CHUNK)
        out_rows = slice(hf * half + j * HG_CHUNK, hf * half + (j + 1) * HG_CHUNK)
        for h in range(HG_HEADS):
            hc = slice(h * HG_DIM, (h + 1) * HG_DIM)
            bc = b_ref[h, rws, :]
            qc = q_ref[h, rws, :]
            kc = k_ref[h, rws, :]
            vc = h_ref[rws, 2 * HG_W + h * HG_DIM:2 * HG_W + (h + 1) * HG_DIM]
            gate = h_ref[rws, 3 * HG_W + h * HG_DIM:3 * HG_W + (h + 1) * HG_DIM]
            b_last = bc[HG_CHUNK - 1:HG_CHUNK]
            st = st_ref[h]
            v_t = jnp.concatenate([vc, zeros_v], axis=0).T.astype(BF16)
            q_hat = (qc * jnp.exp(bc)).astype(BF16)
            k_dec = jnp.concatenate([(kc * jnp.exp(b_last - bc)).astype(BF16), zeros_chunk],
                                    axis=0)
            lhs = jnp.concatenate([a_ref[j * HG_HEADS + h].astype(BF16), q_hat], axis=1)
            rhs = jnp.concatenate([v_t, st.astype(BF16)], axis=1)
            o = _nt_dot(lhs, rhs)
            st_ref[h] = st * jnp.exp(b_last) + _dot(v_t, k_dec)
            o = o * lax.rsqrt(jnp.mean(o * o, axis=-1, keepdims=True) + RMS_EPS)
            y_ref[0, out_rows, hc] = (o * gain * _silu(gate)).astype(y_ref.dtype)

    @pl.when(pl.program_id(1) == 0)
    def _project_first_half():
        project(0, 0, 4 * HG_W)

    gates(0)
    unsafe_0 = cumulative_decay(0)
    for j in range(n_chunk):
        project(1, j * piece, (j + 1) * piece)
        scores(0, j)
    pl.when(unsafe_0)(functools.partial(direct_scores, 0))
    for j in range(n_chunk):
        project(1, 2 * HG_W + j * piece, 2 * HG_W + (j + 1) * piece)
        readout(0, j)
    gates(1)
    unsafe_1 = cumulative_decay(1)
    for j in range(n_chunk):
        project(0, j * piece, (j + 1) * piece, nxt=True)
        scores(1, j)
    pl.when(unsafe_1)(functools.partial(direct_scores, 1))
    for j in range(n_chunk):
        project(0, 2 * HG_W + j * piece, 2 * HG_W + (j + 1) * piece, nxt=True)
        readout(1, j)


def _hgrn_call(xb, w_hg, lb, gain, *, rows):
    bsz, seq, d = xb.shape
    body = functools.partial(_hgrn_body, rows=rows)
    half = rows // 2
    n_chunk = half // HG_CHUNK
    last_half = seq // half - 2
    half_set = [
        pltpu.VMEM((half, 4 * HG_W), F32),
        pltpu.VMEM((HG_HEADS, half, HG_DIM), F32),
        pltpu.VMEM((HG_HEADS, half, HG_DIM), F32),
        pltpu.VMEM((HG_HEADS, half, HG_DIM), F32),
        pltpu.VMEM((n_chunk * HG_HEADS, HG_CHUNK, HG_DIM), F32),
    ]
    return pl.pallas_call(
        body,
        grid=(bsz, seq // rows),
        in_specs=[
            pl.BlockSpec((1, rows, d), lambda b, t: (b, t, 0)),
            pl.BlockSpec((1, half, d), lambda b, t: (b, jnp.minimum(2 * t + 2, last_half), 0)),
            pl.BlockSpec((d, 4 * HG_W), lambda b, t: (0, 0)),
            pl.BlockSpec((1, HG_W), lambda b, t: (0, 0)),
            pl.BlockSpec((1, HG_DIM), lambda b, t: (0, 0)),
        ],
        out_specs=pl.BlockSpec((1, rows, HG_W), lambda b, t: (b, t, 0)),
        out_shape=jax.ShapeDtypeStruct((bsz, seq, HG_W), BF16),
        scratch_shapes=half_set + half_set + [pltpu.VMEM((HG_HEADS, HG_DIM, HG_DIM), F32)],
        compiler_params=pltpu.CompilerParams(
            dimension_semantics=("arbitrary", "arbitrary"), vmem_limit_bytes=VMEM_LIMIT),
        name="hgrn",
    )(xb, xb, w_hg, lb, gain)


def _mla_prep_body(x_ref, w_ref, qn_ref, kvn_ref, wqa_ref, wqb_ref, wk_ref, wv_ref,
                   ct_ref, st_ref, ctt_ref, stt_ref, qt_ref, k_ref, vt_ref, g_ref):
    x = x_ref[0]
    h = _dot(x, w_ref[...])
    cq = h[:, :MLA_Q_RANK]
    ckv = h[:, MLA_Q_RANK:MLA_Q_RANK + MLA_KV_RANK]
    o = MLA_Q_RANK + MLA_KV_RANK
    kr = h[:, o:o + MLA_PAD]
    kr_rot = h[:, o + MLA_PAD:o + 2 * MLA_PAD]
    gate = h[:, o + 2 * MLA_PAD:]

    cqn = (cq * lax.rsqrt(jnp.mean(cq * cq, axis=-1, keepdims=True) + RMS_EPS)
           * qn_ref[...]).astype(BF16)
    c = (ckv * lax.rsqrt(jnp.mean(ckv * ckv, axis=-1, keepdims=True) + RMS_EPS)
         * kvn_ref[...]).astype(BF16)

    scale = LOG2_E / math.sqrt(MLA_QK)
    qa_t = _nt_dot(wqa_ref[...], cqn)
    qb_t = _nt_dot(wqb_ref[...], cqn)
    cq_tab = ctt_ref[0] * scale
    sq_tab = stt_ref[0] * scale
    for hd in range(MLA_HEADS):
        rws = slice(hd * MLA_PAD, (hd + 1) * MLA_PAD)
        qt_ref[0, rws, :] = (qa_t[rws] * cq_tab + qb_t[rws] * sq_tab).astype(qt_ref.dtype)

    kr_roped = (kr * ct_ref[0] + kr_rot * st_ref[0]).astype(BF16)
    ck = jnp.concatenate([c, kr_roped], axis=1)
    k_ref[0] = _dot(ck, wk_ref[...]).astype(k_ref.dtype)

    vt = _nt_dot(wv_ref[...], c).astype(vt_ref.dtype)
    bk = vt_ref.shape[-1]
    for hd in range(MLA_HEADS):
        for sub in range(vt_ref.shape[2]):
            vt_ref[0, hd, sub, :MLA_VDIM, :] = vt[hd * MLA_VDIM:(hd + 1) * MLA_VDIM,
                                                  sub * bk:(sub + 1) * bk]
            vt_ref[0, hd, sub, MLA_VDIM:, :] = jnp.ones((MLA_VROWS - MLA_VDIM, bk), vt_ref.dtype)
    g_ref[0] = _silu(gate).astype(g_ref.dtype)


def _mla_prep_call(xb, w_b, qn, kvn, wqa, wqb, wk, wv, ctab, stab, ctab_t, stab_t, *, rows, bk):
    bsz, seq, d = xb.shape
    hp = MLA_HEADS * MLA_PAD
    const = lambda b, t: (0, 0)
    return pl.pallas_call(
        _mla_prep_body,
        grid=(bsz, seq // rows),
        in_specs=[
            pl.BlockSpec((1, rows, d), lambda b, t: (b, t, 0)),
            pl.BlockSpec(w_b.shape, const),
            pl.BlockSpec((1, MLA_Q_RANK), const),
            pl.BlockSpec((1, MLA_KV_RANK), const),
            pl.BlockSpec(wqa.shape, const),
            pl.BlockSpec(wqb.shape, const),
            pl.BlockSpec(wk.shape, const),
            pl.BlockSpec(wv.shape, const),
            pl.BlockSpec((1, rows, MLA_PAD), lambda b, t: (b, t, 0)),
            pl.BlockSpec((1, rows, MLA_PAD), lambda b, t: (b, t, 0)),
            pl.BlockSpec((1, MLA_PAD, rows), lambda b, t: (b, 0, t)),
            pl.BlockSpec((1, MLA_PAD, rows), lambda b, t: (b, 0, t)),
        ],
        out_specs=[
            pl.BlockSpec((1, hp, rows), lambda b, t: (b, 0, t)),
            pl.BlockSpec((1, rows, hp), lambda b, t: (b, t, 0)),
            pl.BlockSpec((1, MLA_HEADS, rows // bk, MLA_VROWS, bk), lambda b, t: (b, 0, t, 0, 0)),
            pl.BlockSpec((1, rows, MLA_W), lambda b, t: (b, t, 0)),
        ],
        out_shape=[
            jax.ShapeDtypeStruct((bsz, hp, seq), BF16),
            jax.ShapeDtypeStruct((bsz, seq, hp), BF16),
            jax.ShapeDtypeStruct((bsz, MLA_HEADS, seq // bk, MLA_VROWS, bk), BF16),
            jax.ShapeDtypeStruct((bsz, seq, MLA_W), BF16),
        ],
        compiler_params=pltpu.CompilerParams(
            dimension_semantics=("arbitrary", "arbitrary"), vmem_limit_bytes=VMEM_LIMIT),
        name="mla_prep",
    )(xb, w_b, qn, kvn, wqa, wqb, wk, wv, ctab, stab, ctab_t, stab_t)


def _mla_attn_body(qt_ref, k_ref, vt_ref, g_ref, y_ref, *, bq, bk):
    qi = pl.program_id(2)
    n_diag = bq // bk
    assert ATTN_WIDE_SUBS % n_diag == 0
    key_i = lax.broadcasted_iota(jnp.int32, (bk, bq), 0)
    qry_i = lax.broadcasted_iota(jnp.int32, (bk, bq), 1)
    q_t = [qt_ref[0, hh * MLA_PAD:(hh + 1) * MLA_PAD, :] for hh in range(2)]

    def step(first_blk, carry, n_sub, masked):
        carry = list(carry)
        scores = []
        for sub in range(n_sub):
            start = pl.multiple_of((first_blk + sub) * bk, bk)
            scores.append([_dot(k_ref[0, pl.ds(start, bk), hh * MLA_PAD:(hh + 1) * MLA_PAD],
                                q_t[hh]) for hh in range(2)])
        for sub in range(n_sub):
            for hh in range(2):
                m, acc = carry[hh]
                s = scores[sub][hh]
                if masked:
                    s = jnp.where(key_i + sub * bk <= qry_i, s, NEG_BIG)
                m_new = jnp.maximum(m, jnp.max(s, axis=0, keepdims=True))
                p = jnp.exp2(s - m_new).astype(BF16)
                alpha = jnp.exp2(m - m_new)
                acc = alpha * acc + _dot(vt_ref[0, hh, first_blk + sub], p)
                carry[hh] = (m_new, acc)
        return tuple(carry)

    init = tuple((jnp.full((1, bq), NEG_BIG, F32), jnp.zeros((MLA_VROWS, bq), F32))
                 for _ in range(2))
    n_full = qi * n_diag
    n_wide = n_full // ATTN_WIDE_SUBS
    carry = lax.fori_loop(
        0, n_wide, lambda t, c: step(t * ATTN_WIDE_SUBS, c, ATTN_WIDE_SUBS, False), init)
    carry = lax.fori_loop(
        n_wide * (ATTN_WIDE_SUBS // n_diag), qi, lambda t, c: step(t * n_diag, c, n_diag, False),
        carry)
    carry = step(n_full, carry, n_diag, True)
    o_t = jnp.concatenate([acc[:MLA_VDIM] * (1.0 / acc[MLA_VDIM:MLA_VDIM + 1])
                           for (_, acc) in carry], axis=0)
    y_ref[0] = (o_t.T * g_ref[0].astype(F32)).astype(y_ref.dtype)


def _mla_attn_call(qt, k, vt, gate, *, bq):
    bsz, seq, hp = k.shape
    bk = vt.shape[-1]
    body = functools.partial(_mla_attn_body, bq=bq, bk=bk)
    return pl.pallas_call(
        body,
        grid=(bsz, MLA_HEADS // 2, seq // bq),
        in_specs=[
            pl.BlockSpec((1, 2 * MLA_PAD, bq), lambda b, h, i: (b, h, i)),
            pl.BlockSpec((1, seq, 2 * MLA_PAD), lambda b, h, i: (b, 0, h)),
            pl.BlockSpec((1, 2, seq // bk, MLA_VROWS, bk), lambda b, h, i: (b, h, 0, 0, 0)),
            pl.BlockSpec((1, bq, 2 * MLA_VDIM), lambda b, h, i: (b, i, h)),
        ],
        out_specs=pl.BlockSpec((1, bq, 2 * MLA_VDIM), lambda b, h, i: (b, i, h)),
        out_shape=jax.ShapeDtypeStruct((bsz, seq, MLA_W), BF16),
        compiler_params=pltpu.CompilerParams(
            dimension_semantics=("arbitrary", "arbitrary", "arbitrary"),
            vmem_limit_bytes=VMEM_LIMIT),
        name="mla_attn",
    )(qt, k, vt, gate)


def _mem_attn_body(x_ref, w_ref, mem_ref, wk_ref, wv_ref, y_ref, k_ref, v_ref):
    @pl.when(pl.program_id(1) == 0)
    def _project_memory():
        mem = mem_ref[0]
        k_ref[...] = _dot(mem, wk_ref[...]).astype(k_ref.dtype)
        v_ref[...] = _dot(mem, wv_ref[...]).astype(v_ref.dtype)

    x = x_ref[0]
    h = _dot(x, w_ref[...])
    scale = 1.0 / math.sqrt(MEM_DIM)
    for hd in range(MEM_HEADS):
        cols = slice(hd * MEM_DIM, (hd + 1) * MEM_DIM)
        q = (h[:, cols] * scale).astype(BF16)
        s = _nt_dot(q, k_ref[:, cols])
        p = jnp.exp(s - jnp.max(s, axis=-1, keepdims=True))
        inv = 1.0 / jnp.sum(p, axis=-1, keepdims=True)
        o = _dot(p.astype(BF16), v_ref[:, cols]) * inv
        gate = h[:, MEM_W + hd * MEM_DIM:MEM_W + (hd + 1) * MEM_DIM]
        y_ref[0, :, cols] = (o * _silu(gate)).astype(y_ref.dtype)


def _mem_attn_call(xb, w_m, memb, wk, wv, *, rows):
    bsz, seq, d = xb.shape
    mem_len = memb.shape[1]
    const = lambda b, t: (0, 0)
    return pl.pallas_call(
        _mem_attn_body,
        grid=(bsz, seq // rows),
        in_specs=[
            pl.BlockSpec((1, rows, d), lambda b, t: (b, t, 0)),
            pl.BlockSpec(w_m.shape, const),
            pl.BlockSpec((1, mem_len, d), lambda b, t: (b, 0, 0)),
            pl.BlockSpec(wk.shape, const),
            pl.BlockSpec(wv.shape, const),
        ],
        out_specs=pl.BlockSpec((1, rows, MEM_W), lambda b, t: (b, t, 0)),
        out_shape=jax.ShapeDtypeStruct((bsz, seq, MEM_W), BF16),
        scratch_shapes=[pltpu.VMEM((mem_len, MEM_W), BF16), pltpu.VMEM((mem_len, MEM_W), BF16)],
        compiler_params=pltpu.CompilerParams(
            dimension_semantics=("arbitrary", "arbitrary"), vmem_limit_bytes=VMEM_LIMIT),
        name="mem_attn",
    )(xb, w_m, memb, wk, wv)


def _out_ln_body(yh_ref, ya_ref, ym_ref, x_ref, w_ref, g_ref, b_ref, o_ref, ob_ref, *, alpha):
    w = w_ref
    y = (_dot(yh_ref[0], w[:HG_W]) + _dot(ya_ref[0], w[HG_W:HG_W + MLA_W])
         + _dot(ym_ref[0], w[HG_W + MLA_W:]))
    z = alpha * x_ref[0] + y
    mu = jnp.mean(z, axis=-1, keepdims=True)
    zc = z - mu
    var = jnp.mean(zc * zc, axis=-1, keepdims=True)
    out = zc * lax.rsqrt(var + LN_EPS) * g_ref[...] + b_ref[...]
    o_ref[0] = out
    ob_ref[0] = out.astype(ob_ref.dtype)


def _out_ln_call(y_hg, y_mla, y_mem, x, w_out, g, b, *, rows, alpha):
    bsz, seq, d = x.shape
    const = lambda b_, t: (0, 0)
    body = functools.partial(_out_ln_body, alpha=alpha)
    row_blk = lambda width: pl.BlockSpec((1, rows, width), lambda b_, t: (b_, t, 0))
    return pl.pallas_call(
        body,
        grid=(bsz, seq // rows),
        in_specs=[row_blk(HG_W), row_blk(MLA_W), row_blk(MEM_W), row_blk(d),
                  pl.BlockSpec(w_out.shape, const),
                  pl.BlockSpec((1, d), const), pl.BlockSpec((1, d), const)],
        out_specs=[row_blk(d), row_blk(d)],
        out_shape=[jax.ShapeDtypeStruct((bsz, seq, d), F32),
                   jax.ShapeDtypeStruct((bsz, seq, d), BF16)],
        compiler_params=pltpu.CompilerParams(
            dimension_semantics=("arbitrary", "arbitrary"), vmem_limit_bytes=VMEM_LIMIT),
        name="out_ln",
    )(y_hg, y_mla, y_mem, x, w_out, g, b)


def _rotate_half_cols(w):
    half = w.shape[-1] // 2
    return jnp.concatenate([-w[..., half:], w[..., :half]], axis=-1)


def _pad_heads(w, heads, width):
    k = w.shape[0]
    w = w.reshape(k, heads, width)
    w = jnp.pad(w, ((0, 0), (0, 0), (0, MLA_PAD - width)))
    return w.reshape(k, heads * MLA_PAD)


def _mla_weights(w_in_l, w_uq, w_uk, w_uv):
    o = 4 * HG_W
    w_cq = w_in_l[:, o:o + MLA_Q_RANK]
    w_ckv = w_in_l[:, o + MLA_Q_RANK:o + MLA_Q_RANK + MLA_KV_RANK]
    o2 = o + MLA_Q_RANK + MLA_KV_RANK
    w_kr = w_in_l[:, o2:o2 + MLA_ROPE]
    w_gate = w_in_l[:, o2 + MLA_ROPE:o2 + MLA_ROPE + MLA_W]
    lane_pad = lambda w: jnp.pad(w, ((0, 0), (MLA_NOPE, MLA_PAD - MLA_QK)))
    w_b = jnp.concatenate([w_cq, w_ckv, lane_pad(w_kr), lane_pad(_rotate_half_cols(w_kr)), w_gate],
                          axis=1).astype(BF16)

    uq = w_uq.reshape(MLA_Q_RANK, MLA_HEADS, MLA_QK)
    uq_rot = jnp.concatenate([jnp.zeros_like(uq[..., :MLA_NOPE]),
                              _rotate_half_cols(uq[..., MLA_NOPE:])], axis=-1)
    wqa = _pad_heads(uq.reshape(MLA_Q_RANK, -1), MLA_HEADS, MLA_QK).T.astype(BF16)
    wqb = _pad_heads(uq_rot.reshape(MLA_Q_RANK, -1), MLA_HEADS, MLA_QK).T.astype(BF16)

    uk = w_uk.reshape(MLA_KV_RANK, MLA_HEADS, MLA_NOPE)
    top = jnp.pad(uk, ((0, 0), (0, 0), (0, MLA_PAD - MLA_NOPE)))
    pick = jnp.eye(MLA_PAD, dtype=uk.dtype)
    pick = pick * ((jnp.arange(MLA_PAD) >= MLA_NOPE) & (jnp.arange(MLA_PAD) < MLA_QK))
    bot = jnp.broadcast_to(pick[:, None, :], (MLA_PAD, MLA_HEADS, MLA_PAD))
    wk = jnp.concatenate([top, bot], axis=0).reshape(MLA_KV_RANK + MLA_PAD, -1).astype(BF16)

    wv = w_uv.T.astype(BF16)
    return w_b, wqa, wqb, wk, wv


def _rope_tables(positions):
    inv_freq = 1.0 / (10000.0 ** (jnp.arange(0, MLA_ROPE, 2, dtype=F32) / MLA_ROPE))
    ang = positions.astype(F32)[..., None] * inv_freq
    cos, sin = jnp.cos(ang), jnp.sin(ang)
    shape = ang.shape[:-1]
    ctab = jnp.concatenate([jnp.ones(shape + (MLA_NOPE,), F32), cos, cos,
                            jnp.zeros(shape + (MLA_PAD - MLA_QK,), F32)], axis=-1)
    stab = jnp.concatenate([jnp.zeros(shape + (MLA_NOPE,), F32), sin, sin,
                            jnp.zeros(shape + (MLA_PAD - MLA_QK,), F32)], axis=-1)
    return ctab, stab


def _tile_rows(seq, want):
    rows = min(seq, want)
    assert seq % rows == 0
    return rows


def kernel(x, mem, positions, w_in, hgrn_lb_logits, hgrn_out_norm, mla_q_norm, mla_kv_norm,
           w_mla_uq, w_mla_uk, w_mla_uv, w_mem_k, w_mem_v, w_out, ln_g, ln_b):
    depth = w_in.shape[0]
    seq = x.shape[1]
    alpha = (2 * depth) ** 0.25
    ctab, stab = _rope_tables(positions)
    ctab_t, stab_t = jnp.swapaxes(ctab, 1, 2), jnp.swapaxes(stab, 1, 2)
    lb_soft = jax.nn.softmax(hgrn_lb_logits.astype(F32), axis=0)
    lower_bounds = jnp.cumsum(lb_soft, axis=0) - lb_soft[0]
    memb = mem.astype(BF16)
    xb = x.astype(BF16)
    o_mem = 4 * HG_W + MLA_Q_RANK + MLA_KV_RANK + MLA_ROPE + MLA_W

    for l in range(depth):
        w_l = w_in[l]
        w_hg = w_l[:, :4 * HG_W].astype(BF16)
        w_b, wqa, wqb, wk, wv = _mla_weights(w_l, w_mla_uq[l], w_mla_uk[l], w_mla_uv[l])
        w_m = w_l[:, o_mem:].astype(BF16)

        y_hg = _hgrn_call(xb, w_hg, lower_bounds[l][None, :], hgrn_out_norm[l][None, :],
                          rows=_tile_rows(seq, 512))
        blk = _tile_rows(seq, 512)
        qt, k, vt, gate = _mla_prep_call(xb, w_b, mla_q_norm[l][None, :], mla_kv_norm[l][None, :],
                                         wqa, wqb, wk, wv, ctab, stab, ctab_t, stab_t,
                                         rows=blk, bk=blk // 2)
        y_mla = _mla_attn_call(qt, k, vt, gate, bq=blk)
        y_mem = _mem_attn_call(xb, w_m, memb, w_mem_k[l].astype(BF16), w_mem_v[l].astype(BF16),
                               rows=_tile_rows(seq, 512))
        x, xb = _out_ln_call(y_hg, y_mla, y_mem, x, w_out[l].astype(BF16),
                             ln_g[l][None, :], ln_b[l][None, :],
                             rows=_tile_rows(seq, 512), alpha=alpha)
    return x
```

```python
import functools
import math

import jax
import jax.numpy as jnp
from jax import lax
from jax.experimental import pallas as pl
from jax.experimental.pallas import tpu as pltpu

F32 = jnp.float32
BF16 = jnp.bfloat16

LANES = 128

HG_HEADS = 8
HG_DIM = 128
HG_W = HG_HEADS * HG_DIM
HG_CHUNK = 64
HG_HALF = HG_CHUNK // 2
HG_SAFE_DECAY = 60.0

MLA_HEADS = 8
MLA_NOPE = 64
MLA_ROPE = 32
MLA_QK = MLA_NOPE + MLA_ROPE
MLA_VDIM = 64
MLA_Q_RANK = 256
MLA_KV_RANK = 128
MLA_PAD = LANES
MLA_W = MLA_HEADS * MLA_VDIM
MLA_VROWS = MLA_VDIM + 16
ATTN_STEP_SUBS = (4, 2)

MEM_HEADS = 4
MEM_DIM = 128
MEM_W = MEM_HEADS * MEM_DIM

RMS_EPS = 1e-6
LN_EPS = 1e-5
NEG_BIG = -1e30
LOG2_E = math.log2(math.e)

VMEM_LIMIT = 56 * 1024 * 1024


def _nt_dot(a, b):
    return lax.dot_general(a, b, (((1,), (1,)), ((), ())), preferred_element_type=F32)


def _dot(a, b):
    return jnp.dot(a, b, preferred_element_type=F32)


def _silu(x):
    return x * (1.0 / (1.0 + jnp.exp(-x)))


def _hgrn_body(x_ref, xn_ref, w_ref, lb_ref, g_ref, y_ref, *scratch, rows):
    half = rows // 2
    n_chunk = half // HG_CHUNK
    sets = (scratch[0:5], scratch[5:10])
    st_ref = scratch[10]

    @pl.when(pl.program_id(1) == 0)
    def _reset_state():
        st_ref[...] = jnp.zeros_like(st_ref)

    r_i = lax.broadcasted_iota(jnp.int32, (HG_CHUNK, HG_DIM), 0)
    c_i = lax.broadcasted_iota(jnp.int32, (HG_CHUNK, HG_DIM), 1)
    causal = r_i >= c_i
    zeros_chunk = jnp.zeros((HG_CHUNK, HG_DIM), BF16)
    zeros_v = jnp.zeros((HG_CHUNK, HG_DIM), F32)
    gain = g_ref[...]

    piece = 2 * HG_W // n_chunk

    def project(hf, c0, c1, nxt=False):
        x = xn_ref[0] if nxt else x_ref[0, hf * half:(hf + 1) * half, :]
        sets[hf][0][:, c0:c1] = _dot(x, w_ref[:, c0:c1])

    def gates(hf):
        h_ref, q_ref, k_ref = sets[hf][0:3]
        for h in range(HG_HEADS):
            cols = slice(HG_W + h * HG_DIM, HG_W + (h + 1) * HG_DIM)
            z = h_ref[:, cols]
            lb = lb_ref[:, h * HG_DIM:(h + 1) * HG_DIM]
            e = jnp.exp(-z)
            u = 1.0 + e
            h_ref[:, cols] = jnp.log(1.0 + lb * e) - jnp.log(u)
            k_ref[h] = (1.0 - lb) * (1.0 - 1.0 / u)
            q_ref[h] = h_ref[:, h * HG_DIM:(h + 1) * HG_DIM]

    def cumulative_decay(hf):
        h_ref, b_ref = sets[hf][0], sets[hf][3]
        row = lax.broadcasted_iota(jnp.int32, (8, HG_W), 0)
        worst = jnp.zeros((1, HG_W), F32)
        for j in range(n_chunk):
            total = jnp.zeros((1, HG_W), F32)
            marks = []
            for t in range(HG_CHUNK // 8):
                rws = slice(j * HG_CHUNK + 8 * t, j * HG_CHUNK + 8 * (t + 1))
                acc = h_ref[rws, HG_W:2 * HG_W]
                for shift in (1, 2, 4):
                    acc = acc + jnp.where(row >= shift, pltpu.roll(acc, shift, axis=0), 0.0)
                acc = acc + total
                total = acc[7:8]
                for h in range(HG_HEADS):
                    b_ref[h, rws, :] = acc[:, h * HG_DIM:(h + 1) * HG_DIM]
                if 8 * (t + 1) in (HG_HALF, HG_CHUNK):
                    marks.append(total)
            worst = jnp.maximum(worst, jnp.maximum(-marks[0], marks[0] - marks[1]))
        return jnp.max(worst) > HG_SAFE_DECAY

    def scores(hf, j):
        _, q_ref, k_ref, b_ref, a_ref = sets[hf]
        rws = slice(j * HG_CHUNK, (j + 1) * HG_CHUNK)
        for h in range(HG_HEADS):
            bc = b_ref[h, rws, :]
            rel = bc - bc[HG_HALF - 1:HG_HALF]
            q_mid = (q_ref[h, rws, :] * jnp.exp(rel)).astype(BF16)
            k_mid = (k_ref[h, rws, :] * jnp.exp(-rel)).astype(BF16)
            k_pad = jnp.concatenate([k_mid, zeros_chunk], axis=0)
            a_ref[j * HG_HEADS + h] = jnp.where(causal, _nt_dot(q_mid, k_pad), 0.0)

    def direct_scores(hf):
        _, q_ref, k_ref, b_ref, a_ref = sets[hf]

        def per_chunk_head(idx, carry):
            j = idx // HG_HEADS
            h = idx - j * HG_HEADS
            base = pl.multiple_of(j * HG_CHUNK, HG_CHUNK)
            bc = b_ref[h, pl.ds(base, HG_CHUNK), :]
            qc = q_ref[h, pl.ds(base, HG_CHUNK), :]

            def per_key(s, a):
                b_s = b_ref[h, pl.ds(base + s, 1), :]
                k_s = k_ref[h, pl.ds(base + s, 1), :]
                prod = qc * jnp.exp(jnp.minimum(bc - b_s, 0.0)) * k_s
                col = jnp.sum(prod, axis=-1, keepdims=True)
                return jnp.where((c_i == s) & (r_i >= s), col, a)

            a_ref[idx] = lax.fori_loop(0, HG_CHUNK, per_key,
                                       jnp.zeros((HG_CHUNK, HG_DIM), F32))
            return carry

        lax.fori_loop(0, n_chunk * HG_HEADS, per_chunk_head, 0)

    def readout(hf, j):
        h_ref, q_ref, k_ref, b_ref, a_ref = sets[hf]
        rws = slice(j * HG_CHUNK, (j + 1) * HG_CHUNK)
        out_rows = slice(hf * half + j * HG_CHUNK, hf * half + (j + 1) * HG_CHUNK)
        for h in range(HG_HEADS):
            hc = slice(h * HG_DIM, (h + 1) * HG_DIM)
            bc = b_ref[h, rws, :]
            qc = q_ref[h, rws, :]
            kc = k_ref[h, rws, :]
            vc = h_ref[rws, 2 * HG_W + h * HG_DIM:2 * HG_W + (h + 1) * HG_DIM]
            gate = h_ref[rws, 3 * HG_W + h * HG_DIM:3 * HG_W + (h + 1) * HG_DIM]
            b_last = bc[HG_CHUNK - 1:HG_CHUNK]
            st = st_ref[h]
            v_t = jnp.concatenate([vc, zeros_v], axis=0).T.astype(BF16)
            q_hat = (qc * jnp.exp(bc)).astype(BF16)
            k_dec = jnp.concatenate([(kc * jnp.exp(b_last - bc)).astype(BF16), zeros_chunk],
                                    axis=0)
            lhs = jnp.concatenate([a_ref[j * HG_HEADS + h].astype(BF16), q_hat], axis=1)
            rhs = jnp.concatenate([v_t, st.astype(BF16)], axis=1)
            o = _nt_dot(lhs, rhs)
            st_ref[h] = st * jnp.exp(b_last) + _dot(v_t, k_dec)
            o = o * lax.rsqrt(jnp.mean(o * o, axis=-1, keepdims=True) + RMS_EPS)
            y_ref[0, out_rows, hc] = (o * gain * _silu(gate)).astype(y_ref.dtype)

    @pl.when(pl.program_id(1) == 0)
    def _project_first_half():
        project(0, 0, 4 * HG_W)

    gates(0)
    unsafe_0 = cumulative_decay(0)
    for j in range(n_chunk):
        project(1, j * piece, (j + 1) * piece)
        scores(0, j)
    pl.when(unsafe_0)(functools.partial(direct_scores, 0))
    for j in range(n_chunk):
        project(1, 2 * HG_W + j * piece, 2 * HG_W + (j + 1) * piece)
        readout(0, j)
    gates(1)
    unsafe_1 = cumulative_decay(1)
    for j in range(n_chunk):
        project(0, j * piece, (j + 1) * piece, nxt=True)
        scores(1, j)
    pl.when(unsafe_1)(functools.partial(direct_scores, 1))
    for j in range(n_chunk):
        project(0, 2 * HG_W + j * piece, 2 * HG_W + (j + 1) * piece, nxt=True)
        readout(1, j)


def _hgrn_call(xb, w_hg, lb, gain, *, rows):
    bsz, seq, d = xb.shape
    body = functools.partial(_hgrn_body, rows=rows)
    half = rows // 2
    n_chunk = half // HG_CHUNK
    last_half = seq // half - 2
    half_set = [
        pltpu.VMEM((half, 4 * HG_W), F32),
        pltpu.VMEM((HG_HEADS, half, HG_DIM), F32),
        pltpu.VMEM((HG_HEADS, half, HG_DIM), F32),
        pltpu.VMEM((HG_HEADS, half, HG_DIM), F32),
        pltpu.VMEM((n_chunk * HG_HEADS, HG_CHUNK, HG_DIM), F32),
    ]
    return pl.pallas_call(
        body,
        grid=(bsz, seq // rows),
        in_specs=[
            pl.BlockSpec((1, rows, d), lambda b, t: (b, t, 0)),
            pl.BlockSpec((1, half, d), lambda b, t: (b, jnp.minimum(2 * t + 2, last_half), 0)),
            pl.BlockSpec((d, 4 * HG_W), lambda b, t: (0, 0)),
            pl.BlockSpec((1, HG_W), lambda b, t: (0, 0)),
            pl.BlockSpec((1, HG_DIM), lambda b, t: (0, 0)),
        ],
        out_specs=pl.BlockSpec((1, rows, HG_W), lambda b, t: (b, t, 0)),
        out_shape=jax.ShapeDtypeStruct((bsz, seq, HG_W), BF16),
        scratch_shapes=half_set + half_set + [pltpu.VMEM((HG_HEADS, HG_DIM, HG_DIM), F32)],
        compiler_params=pltpu.CompilerParams(
            dimension_semantics=("arbitrary", "arbitrary"), vmem_limit_bytes=VMEM_LIMIT),
        name="hgrn",
    )(xb, xb, w_hg, lb, gain)


def _mla_prep_body(x_ref, w_ref, qn_ref, kvn_ref, wqa_ref, wqb_ref, wk_ref, wv_ref,
                   ct_ref, st_ref, ctt_ref, stt_ref, qt_ref, k_ref, vt_ref, g_ref):
    x = x_ref[0]
    h = _dot(x, w_ref[...])
    cq = h[:, :MLA_Q_RANK]
    ckv = h[:, MLA_Q_RANK:MLA_Q_RANK + MLA_KV_RANK]
    o = MLA_Q_RANK + MLA_KV_RANK
    kr = h[:, o:o + MLA_PAD]
    kr_rot = h[:, o + MLA_PAD:o + 2 * MLA_PAD]
    gate = h[:, o + 2 * MLA_PAD:]

    cqn = (cq * lax.rsqrt(jnp.mean(cq * cq, axis=-1, keepdims=True) + RMS_EPS)
           * qn_ref[...]).astype(BF16)
    c = (ckv * lax.rsqrt(jnp.mean(ckv * ckv, axis=-1, keepdims=True) + RMS_EPS)
         * kvn_ref[...]).astype(BF16)

    scale = LOG2_E / math.sqrt(MLA_QK)
    qa_t = _nt_dot(wqa_ref[...], cqn)
    qb_t = _nt_dot(wqb_ref[...], cqn)
    cq_tab = ctt_ref[0] * scale
    sq_tab = stt_ref[0] * scale
    for hd in range(MLA_HEADS):
        base = hd * MLA_PAD
        rope = slice(base + MLA_NOPE, base + MLA_QK)
        qt_ref[0, base:base + MLA_NOPE, :] = (
            qa_t[base:base + MLA_NOPE] * scale).astype(qt_ref.dtype)
        qt_ref[0, rope, :] = (qa_t[rope] * cq_tab
                              + qb_t[hd * MLA_ROPE:(hd + 1) * MLA_ROPE] * sq_tab).astype(qt_ref.dtype)
        qt_ref[0, base + MLA_QK:base + MLA_PAD, :] = jnp.zeros(
            (MLA_PAD - MLA_QK, qt_ref.shape[-1]), qt_ref.dtype)

    ct128 = jnp.concatenate([ct_ref[0]] * (MLA_PAD // MLA_ROPE), axis=1)
    st128 = jnp.concatenate([st_ref[0]] * (MLA_PAD // MLA_ROPE), axis=1)
    kr_roped = (kr * ct128 + kr_rot * st128).astype(BF16)
    ck = jnp.concatenate([c, kr_roped], axis=1)
    k_ref[0] = _dot(ck, wk_ref[...]).astype(k_ref.dtype)

    vt = _nt_dot(wv_ref[...], c).astype(vt_ref.dtype)
    bk = vt_ref.shape[-1]
    for hd in range(MLA_HEADS):
        for sub in range(vt_ref.shape[2]):
            vt_ref[0, hd, sub, :MLA_VDIM, :] = vt[hd * MLA_VDIM:(hd + 1) * MLA_VDIM,
                                                  sub * bk:(sub + 1) * bk]
            vt_ref[0, hd, sub, MLA_VDIM:, :] = jnp.ones((MLA_VROWS - MLA_VDIM, bk), vt_ref.dtype)
    g_ref[0] = _silu(gate).astype(g_ref.dtype)


def _mla_prep_call(xb, w_b, qn, kvn, wqa, wqb, wk, wv, ctab, stab, ctab_t, stab_t, *, rows, bk):
    bsz, seq, d = xb.shape
    hp = MLA_HEADS * MLA_PAD
    const = lambda b, t: (0, 0)
    return pl.pallas_call(
        _mla_prep_body,
        grid=(bsz, seq // rows),
        in_specs=[
            pl.BlockSpec((1, rows, d), lambda b, t: (b, t, 0)),
            pl.BlockSpec(w_b.shape, const),
            pl.BlockSpec((1, MLA_Q_RANK), const),
            pl.BlockSpec((1, MLA_KV_RANK), const),
            pl.BlockSpec(wqa.shape, const),
            pl.BlockSpec(wqb.shape, const),
            pl.BlockSpec(wk.shape, const),
            pl.BlockSpec(wv.shape, const),
            pl.BlockSpec((1, rows, MLA_ROPE), lambda b, t: (b, t, 0)),
            pl.BlockSpec((1, rows, MLA_ROPE), lambda b, t: (b, t, 0)),
            pl.BlockSpec((1, MLA_ROPE, rows), lambda b, t: (b, 0, t)),
            pl.BlockSpec((1, MLA_ROPE, rows), lambda b, t: (b, 0, t)),
        ],
        out_specs=[
            pl.BlockSpec((1, hp, rows), lambda b, t: (b, 0, t)),
            pl.BlockSpec((1, rows, hp), lambda b, t: (b, t, 0)),
            pl.BlockSpec((1, MLA_HEADS, rows // bk, MLA_VROWS, bk), lambda b, t: (b, 0, t, 0, 0)),
            pl.BlockSpec((1, rows, MLA_W), lambda b, t: (b, t, 0)),
        ],
        out_shape=[
            jax.ShapeDtypeStruct((bsz, hp, seq), BF16),
            jax.ShapeDtypeStruct((bsz, seq, hp), BF16),
            jax.ShapeDtypeStruct((bsz, MLA_HEADS, seq // bk, MLA_VROWS, bk), BF16),
            jax.ShapeDtypeStruct((bsz, seq, MLA_W), BF16),
        ],
        compiler_params=pltpu.CompilerParams(
            dimension_semantics=("arbitrary", "arbitrary"), vmem_limit_bytes=VMEM_LIMIT),
        name="mla_prep",
    )(xb, w_b, qn, kvn, wqa, wqb, wk, wv, ctab, stab, ctab_t, stab_t)


def _mla_attn_body(qt_ref, k_ref, vt_ref, g_ref, y_ref, *, bq, bk):
    qi = pl.program_id(2)
    n_diag = bq // bk
    assert ATTN_STEP_SUBS[-1] == n_diag and all(w % n_diag == 0 for w in ATTN_STEP_SUBS)
    key_i = lax.broadcasted_iota(jnp.int32, (bk, bq), 0)
    qry_i = lax.broadcasted_iota(jnp.int32, (bk, bq), 1)
    q_t = [qt_ref[0, hh * MLA_PAD:(hh + 1) * MLA_PAD, :] for hh in range(2)]

    def step(first_blk, carry, n_sub, masked):
        carry = list(carry)
        scores = []
        for sub in range(n_sub):
            start = pl.multiple_of((first_blk + sub) * bk, bk)
            scores.append([_dot(k_ref[0, pl.ds(start, bk), hh * MLA_PAD:(hh + 1) * MLA_PAD],
                                q_t[hh]) for hh in range(2)])
        for sub in range(n_sub):
            for hh in range(2):
                m, acc = carry[hh]
                s = scores[sub][hh]
                if masked:
                    s = jnp.where(key_i + sub * bk <= qry_i, s, NEG_BIG)
                m_new = jnp.maximum(m, jnp.max(s, axis=0, keepdims=True))
                p = jnp.exp2(s - m_new).astype(BF16)
                alpha = jnp.exp2(m - m_new)
                acc = alpha * acc + _dot(vt_ref[0, hh, first_blk + sub], p)
                carry[hh] = (m_new, acc)
        return tuple(carry)

    init = tuple((jnp.full((1, bq), NEG_BIG, F32), jnp.zeros((MLA_VROWS, bq), F32))
                 for _ in range(2))
    n_full = qi * n_diag
    carry, done = init, 0
    for width in ATTN_STEP_SUBS:
        n_steps = (n_full - done) // width
        carry = lax.fori_loop(
            0, n_steps,
            lambda t, c, width=width, done=done: step(done + t * width, c, width, False), carry)
        done = done + n_steps * width
    carry = step(n_full, carry, n_diag, True)
    o_t = jnp.concatenate([acc[:MLA_VDIM] * (1.0 / acc[MLA_VDIM:MLA_VDIM + 1])
                           for (_, acc) in carry], axis=0)
    y_ref[0] = (o_t.T * g_ref[0].astype(F32)).astype(y_ref.dtype)


def _mla_attn_call(qt, k, vt, gate, *, bq):
    bsz, seq, hp = k.shape
    bk = vt.shape[-1]
    body = functools.partial(_mla_attn_body, bq=bq, bk=bk)
    return pl.pallas_call(
        body,
        grid=(bsz, MLA_HEADS // 2, seq // bq),
        in_specs=[
            pl.BlockSpec((1, 2 * MLA_PAD, bq), lambda b, h, i: (b, h, i)),
            pl.BlockSpec((1, seq, 2 * MLA_PAD), lambda b, h, i: (b, 0, h)),
            pl.BlockSpec((1, 2, seq // bk, MLA_VROWS, bk), lambda b, h, i: (b, h, 0, 0, 0)),
            pl.BlockSpec((1, bq, 2 * MLA_VDIM), lambda b, h, i: (b, i, h)),
        ],
        out_specs=pl.BlockSpec((1, bq, 2 * MLA_VDIM), lambda b, h, i: (b, i, h)),
        out_shape=jax.ShapeDtypeStruct((bsz, seq, MLA_W), BF16),
        compiler_params=pltpu.CompilerParams(
            dimension_semantics=("arbitrary", "arbitrary", "arbitrary"),
            vmem_limit_bytes=VMEM_LIMIT),
        name="mla_attn",
    )(qt, k, vt, gate)


def _mem_attn_body(x_ref, w_ref, mem_ref, wk_ref, wv_ref, y_ref, k_ref, v_ref):
    @pl.when(pl.program_id(1) == 0)
    def _project_memory():
        mem = mem_ref[0]
        k_ref[...] = _dot(mem, wk_ref[...]).astype(k_ref.dtype)
        v_ref[...] = _dot(mem, wv_ref[...]).astype(v_ref.dtype)

    x = x_ref[0]
    scale = 1.0 / math.sqrt(MEM_DIM)
    q_all = _dot(x, w_ref[:, :MEM_W])
    heads = [slice(hd * MEM_DIM, (hd + 1) * MEM_DIM) for hd in range(MEM_HEADS)]
    scores = [_nt_dot((q_all[:, cols] * scale).astype(BF16), k_ref[:, cols]) for cols in heads]
    gate = _dot(x, w_ref[:, MEM_W:])
    for cols, s in zip(heads, scores):
        p = jnp.exp(s - jnp.max(s, axis=-1, keepdims=True))
        inv = 1.0 / jnp.sum(p, axis=-1, keepdims=True)
        o = _dot(p.astype(BF16), v_ref[:, cols]) * inv
        y_ref[0, :, cols] = (o * _silu(gate[:, cols])).astype(y_ref.dtype)


def _mem_attn_call(xb, w_m, memb, wk, wv, *, rows):
    bsz, seq, d = xb.shape
    mem_len = memb.shape[1]
    const = lambda b, t: (0, 0)
    return pl.pallas_call(
        _mem_attn_body,
        grid=(bsz, seq // rows),
        in_specs=[
            pl.BlockSpec((1, rows, d), lambda b, t: (b, t, 0)),
            pl.BlockSpec(w_m.shape, const),
            pl.BlockSpec((1, mem_len, d), lambda b, t: (b, 0, 0)),
            pl.BlockSpec(wk.shape, const),
            pl.BlockSpec(wv.shape, const),
        ],
        out_specs=pl.BlockSpec((1, rows, MEM_W), lambda b, t: (b, t, 0)),
        out_shape=jax.ShapeDtypeStruct((bsz, seq, MEM_W), BF16),
        scratch_shapes=[pltpu.VMEM((mem_len, MEM_W), BF16), pltpu.VMEM((mem_len, MEM_W), BF16)],
        compiler_params=pltpu.CompilerParams(
            dimension_semantics=("arbitrary", "arbitrary"), vmem_limit_bytes=VMEM_LIMIT),
        name="mem_attn",
    )(xb, w_m, memb, wk, wv)


def _out_ln_body(yh_ref, ya_ref, ym_ref, x_ref, w_ref, g_ref, b_ref, o_ref, ob_ref, *, alpha):
    w = w_ref
    half = x_ref.shape[1] // 2
    halves = [slice(r * half, (r + 1) * half) for r in range(2)]
    ys = [_dot(yh_ref[0, rs], w[:HG_W]) + _dot(ya_ref[0, rs], w[HG_W:HG_W + MLA_W])
          + _dot(ym_ref[0, rs], w[HG_W + MLA_W:]) for rs in halves]
    for rs, y in zip(halves, ys):
        z = alpha * x_ref[0, rs] + y
        mu = jnp.mean(z, axis=-1, keepdims=True)
        zc = z - mu
        var = jnp.mean(zc * zc, axis=-1, keepdims=True)
        out = zc * lax.rsqrt(var + LN_EPS) * g_ref[...] + b_ref[...]
        o_ref[0, rs] = out
        ob_ref[0, rs] = out.astype(ob_ref.dtype)


def _out_ln_call(y_hg, y_mla, y_mem, x, w_out, g, b, *, rows, alpha):
    bsz, seq, d = x.shape
    const = lambda b_, t: (0, 0)
    body = functools.partial(_out_ln_body, alpha=alpha)
    row_blk = lambda width: pl.BlockSpec((1, rows, width), lambda b_, t: (b_, t, 0))
    return pl.pallas_call(
        body,
        grid=(bsz, seq // rows),
        in_specs=[row_blk(HG_W), row_blk(MLA_W), row_blk(MEM_W), row_blk(d),
                  pl.BlockSpec(w_out.shape, const),
                  pl.BlockSpec((1, d), const), pl.BlockSpec((1, d), const)],
        out_specs=[row_blk(d), row_blk(d)],
        out_shape=[jax.ShapeDtypeStruct((bsz, seq, d), F32),
                   jax.ShapeDtypeStruct((bsz, seq, d), BF16)],
        compiler_params=pltpu.CompilerParams(
            dimension_semantics=("arbitrary", "arbitrary"), vmem_limit_bytes=VMEM_LIMIT),
        name="out_ln",
    )(y_hg, y_mla, y_mem, x, w_out, g, b)


def _rotate_half_cols(w):
    half = w.shape[-1] // 2
    return jnp.concatenate([-w[..., half:], w[..., :half]], axis=-1)


def _pad_heads(w, heads, width):
    k = w.shape[0]
    w = w.reshape(k, heads, width)
    w = jnp.pad(w, ((0, 0), (0, 0), (0, MLA_PAD - width)))
    return w.reshape(k, heads * MLA_PAD)


def _mla_weights(w_in_l, w_uq, w_uk, w_uv):
    o = 4 * HG_W
    w_cq = w_in_l[:, o:o + MLA_Q_RANK]
    w_ckv = w_in_l[:, o + MLA_Q_RANK:o + MLA_Q_RANK + MLA_KV_RANK]
    o2 = o + MLA_Q_RANK + MLA_KV_RANK
    w_kr = w_in_l[:, o2:o2 + MLA_ROPE]
    w_gate = w_in_l[:, o2 + MLA_ROPE:o2 + MLA_ROPE + MLA_W]
    lane_pad = lambda w: jnp.pad(w, ((0, 0), (MLA_NOPE, MLA_PAD - MLA_QK)))
    w_b = jnp.concatenate([w_cq, w_ckv, lane_pad(w_kr), lane_pad(_rotate_half_cols(w_kr)), w_gate],
                          axis=1).astype(BF16)

    uq = w_uq.reshape(MLA_Q_RANK, MLA_HEADS, MLA_QK)
    uq_rot = jnp.concatenate([jnp.zeros_like(uq[..., :MLA_NOPE]),
                              _rotate_half_cols(uq[..., MLA_NOPE:])], axis=-1)
    wqa = _pad_heads(uq.reshape(MLA_Q_RANK, -1), MLA_HEADS, MLA_QK).T.astype(BF16)
    wqb = uq_rot[..., MLA_NOPE:].reshape(MLA_Q_RANK, -1).T.astype(BF16)

    uk = w_uk.reshape(MLA_KV_RANK, MLA_HEADS, MLA_NOPE)
    top = jnp.pad(uk, ((0, 0), (0, 0), (0, MLA_PAD - MLA_NOPE)))
    pick = jnp.eye(MLA_PAD, dtype=uk.dtype)
    pick = pick * ((jnp.arange(MLA_PAD) >= MLA_NOPE) & (jnp.arange(MLA_PAD) < MLA_QK))
    bot = jnp.broadcast_to(pick[:, None, :], (MLA_PAD, MLA_HEADS, MLA_PAD))
    wk = jnp.concatenate([top, bot], axis=0).reshape(MLA_KV_RANK + MLA_PAD, -1).astype(BF16)

    wv = w_uv.T.astype(BF16)
    return w_b, wqa, wqb, wk, wv


def _rope_tables(positions):
    inv_freq = 1.0 / (10000.0 ** (jnp.arange(0, MLA_ROPE, 2, dtype=F32) / MLA_ROPE))
    ang = positions.astype(F32)[..., None] * inv_freq
    cos, sin = jnp.cos(ang), jnp.sin(ang)
    return jnp.concatenate([cos, cos], axis=-1), jnp.concatenate([sin, sin], axis=-1)


def _tile_rows(seq, want):
    rows = min(seq, want)
    assert seq % rows == 0
    return rows


def kernel(x, mem, positions, w_in, hgrn_lb_logits, hgrn_out_norm, mla_q_norm, mla_kv_norm,
           w_mla_uq, w_mla_uk, w_mla_uv, w_mem_k, w_mem_v, w_out, ln_g, ln_b):
    depth = w_in.shape[0]
    seq = x.shape[1]
    alpha = (2 * depth) ** 0.25
    ctab, stab = _rope_tables(positions)
    ctab_t, stab_t = jnp.swapaxes(ctab, 1, 2), jnp.swapaxes(stab, 1, 2)
    lb_soft = jax.nn.softmax(hgrn_lb_logits.astype(F32), axis=0)
    lower_bounds = jnp.cumsum(lb_soft, axis=0) - lb_soft[0]
    memb = mem.astype(BF16)
    xb = x.astype(BF16)
    o_mem = 4 * HG_W + MLA_Q_RANK + MLA_KV_RANK + MLA_ROPE + MLA_W

    for l in range(depth):
        w_l = w_in[l]
        w_hg = w_l[:, :4 * HG_W].astype(BF16)
        w_b, wqa, wqb, wk, wv = _mla_weights(w_l, w_mla_uq[l], w_mla_uk[l], w_mla_uv[l])
        w_m = w_l[:, o_mem:].astype(BF16)

        y_hg = _hgrn_call(xb, w_hg, lower_bounds[l][None, :], hgrn_out_norm[l][None, :],
                          rows=_tile_rows(seq, 512))
        blk = _tile_rows(seq, 512)
        qt, k, vt, gate = _mla_prep_call(xb, w_b, mla_q_norm[l][None, :], mla_kv_norm[l][None, :],
                                         wqa, wqb, wk, wv, ctab, stab, ctab_t, stab_t,
                                         rows=blk, bk=blk // 2)
        y_mla = _mla_attn_call(qt, k, vt, gate, bq=blk)
        y_mem = _mem_attn_call(xb, w_m, memb, w_mem_k[l].astype(BF16), w_mem_v[l].astype(BF16),
                               rows=_tile_rows(seq, 512))
        x, xb = _out_ln_call(y_hg, y_mla, y_mem, x, w_out[l].astype(BF16),
                             ln_g[l][None, :], ln_b[l][None, :],
                             rows=_tile_rows(seq, 512), alpha=alpha)
    return x
```

```python
import functools
import math

import jax
import jax.numpy as jnp
from jax import lax
from jax.experimental import pallas as pl
from jax.experimental.pallas import tpu as pltpu

F32 = jnp.float32
BF16 = jnp.bfloat16

LANES = 128

HG_HEADS = 8
HG_DIM = 128
HG_W = HG_HEADS * HG_DIM
HG_CHUNK = 64
HG_HALF = HG_CHUNK // 2
HG_SAFE_DECAY = 60.0

MLA_HEADS = 8
MLA_NOPE = 64
MLA_ROPE = 32
MLA_QK = MLA_NOPE + MLA_ROPE
MLA_VDIM = 64
MLA_Q_RANK = 256
MLA_KV_RANK = 128
MLA_PAD = LANES
MLA_W = MLA_HEADS * MLA_VDIM
MLA_VROWS = MLA_VDIM + 16
ATTN_STEP_SUBS = (4, 2)

MEM_HEADS = 4
MEM_DIM = 128
MEM_W = MEM_HEADS * MEM_DIM

RMS_EPS = 1e-6
LN_EPS = 1e-5
NEG_BIG = -1e30
LOG2_E = math.log2(math.e)

VMEM_LIMIT = 56 * 1024 * 1024


def _nt_dot(a, b):
    return lax.dot_general(a, b, (((1,), (1,)), ((), ())), preferred_element_type=F32)


def _dot(a, b):
    return jnp.dot(a, b, preferred_element_type=F32)


def _silu(x):
    return x * (1.0 / (1.0 + jnp.exp(-x)))


def _hgrn_body(x_ref, xn_ref, w_ref, lb_ref, g_ref, y_ref, *scratch, rows):
    half = rows // 2
    n_chunk = half // HG_CHUNK
    sets = (scratch[0:5], scratch[5:10])
    st_ref = scratch[10]

    @pl.when(pl.program_id(1) == 0)
    def _reset_state():
        st_ref[...] = jnp.zeros_like(st_ref)

    r_i = lax.broadcasted_iota(jnp.int32, (HG_CHUNK, HG_DIM), 0)
    c_i = lax.broadcasted_iota(jnp.int32, (HG_CHUNK, HG_DIM), 1)
    causal = r_i >= c_i
    zeros_chunk = jnp.zeros((HG_CHUNK, HG_DIM), BF16)
    zeros_v = jnp.zeros((HG_CHUNK, HG_DIM), F32)
    gain = g_ref[...]

    piece = 2 * HG_W // n_chunk

    def project(hf, c0, c1, nxt=False):
        x = xn_ref[0] if nxt else x_ref[0, hf * half:(hf + 1) * half, :]
        sets[hf][0][:, c0:c1] = _dot(x, w_ref[:, c0:c1])

    def gates(hf):
        h_ref, q_ref, k_ref = sets[hf][0:3]
        for h in range(HG_HEADS):
            cols = slice(HG_W + h * HG_DIM, HG_W + (h + 1) * HG_DIM)
            z = h_ref[:, cols]
            lb = lb_ref[:, h * HG_DIM:(h + 1) * HG_DIM]
            e = jnp.exp(-z)
            u = 1.0 + e
            h_ref[:, cols] = jnp.log(1.0 + lb * e) - jnp.log(u)
            k_ref[h] = (1.0 - lb) * (1.0 - 1.0 / u)
            q_ref[h] = h_ref[:, h * HG_DIM:(h + 1) * HG_DIM]

    def cumulative_decay(hf):
        h_ref, b_ref = sets[hf][0], sets[hf][3]
        row = lax.broadcasted_iota(jnp.int32, (8, HG_W), 0)
        worst = jnp.zeros((1, HG_W), F32)
        for j in range(n_chunk):
            total = jnp.zeros((1, HG_W), F32)
            marks = []
            for t in range(HG_CHUNK // 8):
                rws = slice(j * HG_CHUNK + 8 * t, j * HG_CHUNK + 8 * (t + 1))
                acc = h_ref[rws, HG_W:2 * HG_W]
                for shift in (1, 2, 4):
                    acc = acc + jnp.where(row >= shift, pltpu.roll(acc, shift, axis=0), 0.0)
                acc = acc + total
                total = acc[7:8]
                for h in range(HG_HEADS):
                    b_ref[h, rws, :] = acc[:, h * HG_DIM:(h + 1) * HG_DIM]
                if 8 * (t + 1) in (HG_HALF, HG_CHUNK):
                    marks.append(total)
            worst = jnp.maximum(worst, jnp.maximum(-marks[0], marks[0] - marks[1]))
        return jnp.max(worst) > HG_SAFE_DECAY

    def scores(hf, j):
        _, q_ref, k_ref, b_ref, a_ref = sets[hf]
        rws = slice(j * HG_CHUNK, (j + 1) * HG_CHUNK)
        for h in range(HG_HEADS):
            bc = b_ref[h, rws, :]
            rel = bc - bc[HG_HALF - 1:HG_HALF]
            q_mid = (q_ref[h, rws, :] * jnp.exp(rel)).astype(BF16)
            k_mid = (k_ref[h, rws, :] * jnp.exp(-rel)).astype(BF16)
            k_pad = jnp.concatenate([k_mid, zeros_chunk], axis=0)
            a_ref[j * HG_HEADS + h] = jnp.where(causal, _nt_dot(q_mid, k_pad), 0.0)

    def direct_scores(hf):
        _, q_ref, k_ref, b_ref, a_ref = sets[hf]

        def per_chunk_head(idx, carry):
            j = idx // HG_HEADS
            h = idx - j * HG_HEADS
            base = pl.multiple_of(j * HG_CHUNK, HG_CHUNK)
            bc = b_ref[h, pl.ds(base, HG_CHUNK), :]
            qc = q_ref[h, pl.ds(base, HG_CHUNK), :]

            def per_key(s, a):
                b_s = b_ref[h, pl.ds(base + s, 1), :]
                k_s = k_ref[h, pl.ds(base + s, 1), :]
                prod = qc * jnp.exp(jnp.minimum(bc - b_s, 0.0)) * k_s
                col = jnp.sum(prod, axis=-1, keepdims=True)
                return jnp.where((c_i == s) & (r_i >= s), col, a)

            a_ref[idx] = lax.fori_loop(0, HG_CHUNK, per_key,
                                       jnp.zeros((HG_CHUNK, HG_DIM), F32))
            return carry

        lax.fori_loop(0, n_chunk * HG_HEADS, per_chunk_head, 0)

    def readout(hf, j):
        h_ref, q_ref, k_ref, b_ref, a_ref = sets[hf]
        rws = slice(j * HG_CHUNK, (j + 1) * HG_CHUNK)
        out_rows = slice(hf * half + j * HG_CHUNK, hf * half + (j + 1) * HG_CHUNK)
        for h in range(HG_HEADS):
            hc = slice(h * HG_DIM, (h + 1) * HG_DIM)
            bc = b_ref[h, rws, :]
            qc = q_ref[h, rws, :]
            kc = k_ref[h, rws, :]
            vc = h_ref[rws, 2 * HG_W + h * HG_DIM:2 * HG_W + (h + 1) * HG_DIM]
            gate = h_ref[rws, 3 * HG_W + h * HG_DIM:3 * HG_W + (h + 1) * HG_DIM]
            b_last = bc[HG_CHUNK - 1:HG_CHUNK]
            st = st_ref[h]
            v_t = jnp.concatenate([vc, zeros_v], axis=0).T.astype(BF16)
            q_hat = (qc * jnp.exp(bc)).astype(BF16)
            k_dec = jnp.concatenate([(kc * jnp.exp(b_last - bc)).astype(BF16), zeros_chunk],
                                    axis=0)
            lhs = jnp.concatenate([a_ref[j * HG_HEADS + h].astype(BF16), q_hat], axis=1)
            rhs = jnp.concatenate([v_t, st.astype(BF16)], axis=1)
            o = _nt_dot(lhs, rhs)
            st_ref[h] = st * jnp.exp(b_last) + _dot(v_t, k_dec)
            o = o * lax.rsqrt(jnp.mean(o * o, axis=-1, keepdims=True) + RMS_EPS)
            y_ref[0, out_rows, hc] = (o * gain * _silu(gate)).astype(y_ref.dtype)

    @pl.when(pl.program_id(1) == 0)
    def _project_first_half():
        project(0, 0, 4 * HG_W)

    gates(0)
    unsafe_0 = cumulative_decay(0)
    for j in range(n_chunk):
        project(1, j * piece, (j + 1) * piece)
        scores(0, j)
    pl.when(unsafe_0)(functools.partial(direct_scores, 0))
    for j in range(n_chunk):
        project(1, 2 * HG_W + j * piece, 2 * HG_W + (j + 1) * piece)
        readout(0, j)
    gates(1)
    unsafe_1 = cumulative_decay(1)
    for j in range(n_chunk):
        project(0, j * piece, (j + 1) * piece, nxt=True)
        scores(1, j)
    pl.when(unsafe_1)(functools.partial(direct_scores, 1))
    for j in range(n_chunk):
        project(0, 2 * HG_W + j * piece, 2 * HG_W + (j + 1) * piece, nxt=True)
        readout(1, j)


def _hgrn_call(xb, w_in_b, layer, lb, gain, *, rows):
    bsz, seq, d = xb.shape
    body = functools.partial(_hgrn_body, rows=rows)
    half = rows // 2
    n_chunk = half // HG_CHUNK
    last_half = seq // half - 2
    half_set = [
        pltpu.VMEM((half, 4 * HG_W), F32),
        pltpu.VMEM((HG_HEADS, half, HG_DIM), F32),
        pltpu.VMEM((HG_HEADS, half, HG_DIM), F32),
        pltpu.VMEM((HG_HEADS, half, HG_DIM), F32),
        pltpu.VMEM((n_chunk * HG_HEADS, HG_CHUNK, HG_DIM), F32),
    ]
    return pl.pallas_call(
        body,
        grid=(bsz, seq // rows),
        in_specs=[
            pl.BlockSpec((1, rows, d), lambda b, t: (b, t, 0)),
            pl.BlockSpec((1, half, d), lambda b, t: (b, jnp.minimum(2 * t + 2, last_half), 0)),
            pl.BlockSpec((None, d, 4 * HG_W), lambda b, t: (layer, 0, 0)),
            pl.BlockSpec((1, HG_W), lambda b, t: (0, 0)),
            pl.BlockSpec((1, HG_DIM), lambda b, t: (0, 0)),
        ],
        out_specs=pl.BlockSpec((1, rows, HG_W), lambda b, t: (b, t, 0)),
        out_shape=jax.ShapeDtypeStruct((bsz, seq, HG_W), BF16),
        scratch_shapes=half_set + half_set + [pltpu.VMEM((HG_HEADS, HG_DIM, HG_DIM), F32)],
        compiler_params=pltpu.CompilerParams(
            dimension_semantics=("arbitrary", "arbitrary"), vmem_limit_bytes=VMEM_LIMIT),
        name="hgrn",
    )(xb, xb, w_in_b, lb, gain)


def _mla_prep_body(x_ref, w_ref, qn_ref, kvn_ref, wqa_ref, wqb_ref, wk_ref, wv_ref,
                   ct_ref, st_ref, ctt_ref, stt_ref, qt_ref, k_ref, vt_ref, g_ref):
    x = x_ref[0]
    h = _dot(x, w_ref[...])
    cq = h[:, :MLA_Q_RANK]
    ckv = h[:, MLA_Q_RANK:MLA_Q_RANK + MLA_KV_RANK]
    o = MLA_Q_RANK + MLA_KV_RANK
    kr = h[:, o:o + MLA_PAD]
    kr_rot = h[:, o + MLA_PAD:o + 2 * MLA_PAD]
    gate = h[:, o + 2 * MLA_PAD:]

    cqn = (cq * lax.rsqrt(jnp.mean(cq * cq, axis=-1, keepdims=True) + RMS_EPS)
           * qn_ref[...]).astype(BF16)
    c = (ckv * lax.rsqrt(jnp.mean(ckv * ckv, axis=-1, keepdims=True) + RMS_EPS)
         * kvn_ref[...]).astype(BF16)

    scale = LOG2_E / math.sqrt(MLA_QK)
    qa_t = _nt_dot(wqa_ref[...], cqn)
    qb_t = _nt_dot(wqb_ref[...], cqn)
    cq_tab = ctt_ref[0] * scale
    sq_tab = stt_ref[0] * scale
    for hd in range(MLA_HEADS):
        base = hd * MLA_PAD
        rope = slice(base + MLA_NOPE, base + MLA_QK)
        qt_ref[0, base:base + MLA_NOPE, :] = (
            qa_t[base:base + MLA_NOPE] * scale).astype(qt_ref.dtype)
        qt_ref[0, rope, :] = (qa_t[rope] * cq_tab
                              + qb_t[hd * MLA_ROPE:(hd + 1) * MLA_ROPE] * sq_tab).astype(qt_ref.dtype)
        qt_ref[0, base + MLA_QK:base + MLA_PAD, :] = jnp.zeros(
            (MLA_PAD - MLA_QK, qt_ref.shape[-1]), qt_ref.dtype)

    ct128 = jnp.concatenate([ct_ref[0]] * (MLA_PAD // MLA_ROPE), axis=1)
    st128 = jnp.concatenate([st_ref[0]] * (MLA_PAD // MLA_ROPE), axis=1)
    kr_roped = (kr * ct128 + kr_rot * st128).astype(BF16)
    ck = jnp.concatenate([c, kr_roped], axis=1)
    k_ref[0] = _dot(ck, wk_ref[...]).astype(k_ref.dtype)

    vt = _nt_dot(wv_ref[...], c).astype(vt_ref.dtype)
    bk = vt_ref.shape[-1]
    for hd in range(MLA_HEADS):
        for sub in range(vt_ref.shape[2]):
            vt_ref[0, hd, sub, :MLA_VDIM, :] = vt[hd * MLA_VDIM:(hd + 1) * MLA_VDIM,
                                                  sub * bk:(sub + 1) * bk]
            vt_ref[0, hd, sub, MLA_VDIM:, :] = jnp.ones((MLA_VROWS - MLA_VDIM, bk), vt_ref.dtype)
    g_ref[0] = _silu(gate).astype(g_ref.dtype)


def _mla_prep_call(xb, w_b, qn, kvn, wqa, wqb, wk, wv, ctab, stab, ctab_t, stab_t, *, rows, bk):
    bsz, seq, d = xb.shape
    hp = MLA_HEADS * MLA_PAD
    const = lambda b, t: (0, 0)
    return pl.pallas_call(
        _mla_prep_body,
        grid=(bsz, seq // rows),
        in_specs=[
            pl.BlockSpec((1, rows, d), lambda b, t: (b, t, 0)),
            pl.BlockSpec(w_b.shape, const),
            pl.BlockSpec((1, MLA_Q_RANK), const),
            pl.BlockSpec((1, MLA_KV_RANK), const),
            pl.BlockSpec(wqa.shape, const),
            pl.BlockSpec(wqb.shape, const),
            pl.BlockSpec(wk.shape, const),
            pl.BlockSpec(wv.shape, const),
            pl.BlockSpec((1, rows, MLA_ROPE), lambda b, t: (b, t, 0)),
            pl.BlockSpec((1, rows, MLA_ROPE), lambda b, t: (b, t, 0)),
            pl.BlockSpec((1, MLA_ROPE, rows), lambda b, t: (b, 0, t)),
            pl.BlockSpec((1, MLA_ROPE, rows), lambda b, t: (b, 0, t)),
        ],
        out_specs=[
            pl.BlockSpec((1, hp, rows), lambda b, t: (b, 0, t)),
            pl.BlockSpec((1, rows, hp), lambda b, t: (b, t, 0)),
            pl.BlockSpec((1, MLA_HEADS, rows // bk, MLA_VROWS, bk), lambda b, t: (b, 0, t, 0, 0)),
            pl.BlockSpec((1, rows, MLA_W), lambda b, t: (b, t, 0)),
        ],
        out_shape=[
            jax.ShapeDtypeStruct((bsz, hp, seq), BF16),
            jax.ShapeDtypeStruct((bsz, seq, hp), BF16),
            jax.ShapeDtypeStruct((bsz, MLA_HEADS, seq // bk, MLA_VROWS, bk), BF16),
            jax.ShapeDtypeStruct((bsz, seq, MLA_W), BF16),
        ],
        compiler_params=pltpu.CompilerParams(
            dimension_semantics=("arbitrary", "arbitrary"), vmem_limit_bytes=VMEM_LIMIT),
        name="mla_prep",
    )(xb, w_b, qn, kvn, wqa, wqb, wk, wv, ctab, stab, ctab_t, stab_t)


def _mla_attn_body(qt_ref, k_ref, vt_ref, g_ref, y_ref, *, bq, bk):
    qi = pl.program_id(2)
    n_diag = bq // bk
    assert ATTN_STEP_SUBS[-1] == n_diag and all(w % n_diag == 0 for w in ATTN_STEP_SUBS)
    key_i = lax.broadcasted_iota(jnp.int32, (bk, bq), 0)
    qry_i = lax.broadcasted_iota(jnp.int32, (bk, bq), 1)
    q_t = [qt_ref[0, hh * MLA_PAD:(hh + 1) * MLA_PAD, :] for hh in range(2)]

    def step(first_blk, carry, n_sub, masked):
        carry = list(carry)
        scores = []
        for sub in range(n_sub):
            start = pl.multiple_of((first_blk + sub) * bk, bk)
            scores.append([_dot(k_ref[0, pl.ds(start, bk), hh * MLA_PAD:(hh + 1) * MLA_PAD],
                                q_t[hh]) for hh in range(2)])
        for sub in range(n_sub):
            for hh in range(2):
                m, acc = carry[hh]
                s = scores[sub][hh]
                if masked:
                    s = jnp.where(key_i + sub * bk <= qry_i, s, NEG_BIG)
                m_new = jnp.maximum(m, jnp.max(s, axis=0, keepdims=True))
                p = jnp.exp2(s - m_new).astype(BF16)
                alpha = jnp.exp2(m - m_new)
                acc = alpha * acc + _dot(vt_ref[0, hh, first_blk + sub], p)
                carry[hh] = (m_new, acc)
        return tuple(carry)

    init = tuple((jnp.full((1, bq), NEG_BIG, F32), jnp.zeros((MLA_VROWS, bq), F32))
                 for _ in range(2))
    n_full = qi * n_diag
    carry, done = init, 0
    for width in ATTN_STEP_SUBS:
        n_steps = (n_full - done) // width
        carry = lax.fori_loop(
            0, n_steps,
            lambda t, c, width=width, done=done: step(done + t * width, c, width, False), carry)
        done = done + n_steps * width
    carry = step(n_full, carry, n_diag, True)
    o_t = jnp.concatenate([acc[:MLA_VDIM] * (1.0 / acc[MLA_VDIM:MLA_VDIM + 1])
                           for (_, acc) in carry], axis=0)
    y_ref[0] = (o_t.T * g_ref[0].astype(F32)).astype(y_ref.dtype)


def _mla_attn_call(qt, k, vt, gate, *, bq):
    bsz, seq, hp = k.shape
    bk = vt.shape[-1]
    body = functools.partial(_mla_attn_body, bq=bq, bk=bk)
    return pl.pallas_call(
        body,
        grid=(bsz, MLA_HEADS // 2, seq // bq),
        in_specs=[
            pl.BlockSpec((1, 2 * MLA_PAD, bq), lambda b, h, i: (b, h, i)),
            pl.BlockSpec((1, seq, 2 * MLA_PAD), lambda b, h, i: (b, 0, h)),
            pl.BlockSpec((1, 2, seq // bk, MLA_VROWS, bk), lambda b, h, i: (b, h, 0, 0, 0)),
            pl.BlockSpec((1, bq, 2 * MLA_VDIM), lambda b, h, i: (b, i, h)),
        ],
        out_specs=pl.BlockSpec((1, bq, 2 * MLA_VDIM), lambda b, h, i: (b, i, h)),
        out_shape=jax.ShapeDtypeStruct((bsz, seq, MLA_W), BF16),
        compiler_params=pltpu.CompilerParams(
            dimension_semantics=("arbitrary", "arbitrary", "arbitrary"),
            vmem_limit_bytes=VMEM_LIMIT),
        name="mla_attn",
    )(qt, k, vt, gate)


def _mem_attn_body(x_ref, w_ref, mem_ref, wk_ref, wv_ref, y_ref, k_ref, v_ref):
    @pl.when(pl.program_id(1) == 0)
    def _project_memory():
        mem = mem_ref[0]
        k_ref[...] = _dot(mem, wk_ref[...]).astype(k_ref.dtype)
        v_ref[...] = _dot(mem, wv_ref[...]).astype(v_ref.dtype)

    x = x_ref[0]
    scale = 1.0 / math.sqrt(MEM_DIM)
    q_all = _dot(x, w_ref[:, :MEM_W])
    heads = [slice(hd * MEM_DIM, (hd + 1) * MEM_DIM) for hd in range(MEM_HEADS)]
    scores = [_nt_dot((q_all[:, cols] * scale).astype(BF16), k_ref[:, cols]) for cols in heads]
    gate = _dot(x, w_ref[:, MEM_W:])
    for cols, s in zip(heads, scores):
        p = jnp.exp(s - jnp.max(s, axis=-1, keepdims=True))
        inv = 1.0 / jnp.sum(p, axis=-1, keepdims=True)
        o = _dot(p.astype(BF16), v_ref[:, cols]) * inv
        y_ref[0, :, cols] = (o * _silu(gate[:, cols])).astype(y_ref.dtype)


def _mem_attn_call(xb, w_m, memb, wk, wv, *, rows):
    bsz, seq, d = xb.shape
    mem_len = memb.shape[1]
    const = lambda b, t: (0, 0)
    return pl.pallas_call(
        _mem_attn_body,
        grid=(bsz, seq // rows),
        in_specs=[
            pl.BlockSpec((1, rows, d), lambda b, t: (b, t, 0)),
            pl.BlockSpec(w_m.shape, const),
            pl.BlockSpec((1, mem_len, d), lambda b, t: (b, 0, 0)),
            pl.BlockSpec(wk.shape, const),
            pl.BlockSpec(wv.shape, const),
        ],
        out_specs=pl.BlockSpec((1, rows, MEM_W), lambda b, t: (b, t, 0)),
        out_shape=jax.ShapeDtypeStruct((bsz, seq, MEM_W), BF16),
        scratch_shapes=[pltpu.VMEM((mem_len, MEM_W), BF16), pltpu.VMEM((mem_len, MEM_W), BF16)],
        compiler_params=pltpu.CompilerParams(
            dimension_semantics=("arbitrary", "arbitrary"), vmem_limit_bytes=VMEM_LIMIT),
        name="mem_attn",
    )(xb, w_m, memb, wk, wv)


def _out_ln_body(yh_ref, ya_ref, ym_ref, x_ref, w_ref, g_ref, b_ref, o_ref, ob_ref, *, alpha):
    w = w_ref
    half = x_ref.shape[1] // 2
    halves = [slice(r * half, (r + 1) * half) for r in range(2)]
    ys = [_dot(yh_ref[0, rs], w[:HG_W]) + _dot(ya_ref[0, rs], w[HG_W:HG_W + MLA_W])
          + _dot(ym_ref[0, rs], w[HG_W + MLA_W:]) for rs in halves]
    for rs, y in zip(halves, ys):
        z = alpha * x_ref[0, rs] + y
        mu = jnp.mean(z, axis=-1, keepdims=True)
        zc = z - mu
        var = jnp.mean(zc * zc, axis=-1, keepdims=True)
        out = zc * lax.rsqrt(var + LN_EPS) * g_ref[...] + b_ref[...]
        o_ref[0, rs] = out
        ob_ref[0, rs] = out.astype(ob_ref.dtype)


def _out_ln_call(y_hg, y_mla, y_mem, x, w_out, g, b, *, rows, alpha):
    bsz, seq, d = x.shape
    const = lambda b_, t: (0, 0)
    body = functools.partial(_out_ln_body, alpha=alpha)
    row_blk = lambda width: pl.BlockSpec((1, rows, width), lambda b_, t: (b_, t, 0))
    return pl.pallas_call(
        body,
        grid=(bsz, seq // rows),
        in_specs=[row_blk(HG_W), row_blk(MLA_W), row_blk(MEM_W), row_blk(d),
                  pl.BlockSpec(w_out.shape, const),
                  pl.BlockSpec((1, d), const), pl.BlockSpec((1, d), const)],
        out_specs=[row_blk(d), row_blk(d)],
        out_shape=[jax.ShapeDtypeStruct((bsz, seq, d), F32),
                   jax.ShapeDtypeStruct((bsz, seq, d), BF16)],
        compiler_params=pltpu.CompilerParams(
            dimension_semantics=("arbitrary", "arbitrary"), vmem_limit_bytes=VMEM_LIMIT),
        name="out_ln",
    )(y_hg, y_mla, y_mem, x, w_out, g, b)


def _rotate_half_cols(w):
    half = w.shape[-1] // 2
    return jnp.concatenate([-w[..., half:], w[..., :half]], axis=-1)


def _pad_heads(w, heads, width):
    k = w.shape[0]
    w = w.reshape(k, heads, width)
    w = jnp.pad(w, ((0, 0), (0, 0), (0, MLA_PAD - width)))
    return w.reshape(k, heads * MLA_PAD)


def _mla_weights(w_in_l, w_uq, w_uk, w_uv):
    o = 4 * HG_W
    w_cq = w_in_l[:, o:o + MLA_Q_RANK]
    w_ckv = w_in_l[:, o + MLA_Q_RANK:o + MLA_Q_RANK + MLA_KV_RANK]
    o2 = o + MLA_Q_RANK + MLA_KV_RANK
    w_kr = w_in_l[:, o2:o2 + MLA_ROPE]
    w_gate = w_in_l[:, o2 + MLA_ROPE:o2 + MLA_ROPE + MLA_W]
    lane_pad = lambda w: jnp.pad(w, ((0, 0), (MLA_NOPE, MLA_PAD - MLA_QK)))
    w_b = jnp.concatenate([w_cq, w_ckv, lane_pad(w_kr), lane_pad(_rotate_half_cols(w_kr)), w_gate],
                          axis=1).astype(BF16)

    uq = w_uq.reshape(MLA_Q_RANK, MLA_HEADS, MLA_QK)
    uq_rot = jnp.concatenate([jnp.zeros_like(uq[..., :MLA_NOPE]),
                              _rotate_half_cols(uq[..., MLA_NOPE:])], axis=-1)
    wqa = _pad_heads(uq.reshape(MLA_Q_RANK, -1), MLA_HEADS, MLA_QK).T.astype(BF16)
    wqb = uq_rot[..., MLA_NOPE:].reshape(MLA_Q_RANK, -1).T.astype(BF16)

    uk = w_uk.reshape(MLA_KV_RANK, MLA_HEADS, MLA_NOPE)
    top = jnp.pad(uk, ((0, 0), (0, 0), (0, MLA_PAD - MLA_NOPE)))
    pick = jnp.eye(MLA_PAD, dtype=uk.dtype)
    pick = pick * ((jnp.arange(MLA_PAD) >= MLA_NOPE) & (jnp.arange(MLA_PAD) < MLA_QK))
    bot = jnp.broadcast_to(pick[:, None, :], (MLA_PAD, MLA_HEADS, MLA_PAD))
    wk = jnp.concatenate([top, bot], axis=0).reshape(MLA_KV_RANK + MLA_PAD, -1).astype(BF16)

    wv = w_uv.T.astype(BF16)
    return w_b, wqa, wqb, wk, wv


def _rope_tables(positions):
    inv_freq = 1.0 / (10000.0 ** (jnp.arange(0, MLA_ROPE, 2, dtype=F32) / MLA_ROPE))
    ang = positions.astype(F32)[..., None] * inv_freq
    cos, sin = jnp.cos(ang), jnp.sin(ang)
    return jnp.concatenate([cos, cos], axis=-1), jnp.concatenate([sin, sin], axis=-1)


def _tile_rows(seq, want):
    rows = min(seq, want)
    assert seq % rows == 0
    return rows


def kernel(x, mem, positions, w_in, hgrn_lb_logits, hgrn_out_norm, mla_q_norm, mla_kv_norm,
           w_mla_uq, w_mla_uk, w_mla_uv, w_mem_k, w_mem_v, w_out, ln_g, ln_b):
    depth = w_in.shape[0]
    seq = x.shape[1]
    alpha = (2 * depth) ** 0.25
    ctab, stab = _rope_tables(positions)
    ctab_t, stab_t = jnp.swapaxes(ctab, 1, 2), jnp.swapaxes(stab, 1, 2)
    lb_soft = jax.nn.softmax(hgrn_lb_logits.astype(F32), axis=0)
    lower_bounds = jnp.cumsum(lb_soft, axis=0) - lb_soft[0]
    memb = mem.astype(BF16)
    xb = x.astype(BF16)
    o_mem = 4 * HG_W + MLA_Q_RANK + MLA_KV_RANK + MLA_ROPE + MLA_W

    w_in_b = w_in.astype(BF16)
    for l in range(depth):
        w_l = w_in_b[l]
        w_b, wqa, wqb, wk, wv = _mla_weights(w_l, w_mla_uq[l], w_mla_uk[l], w_mla_uv[l])
        w_m = w_l[:, o_mem:]

        y_hg = _hgrn_call(xb, w_in_b, l, lower_bounds[l][None, :], hgrn_out_norm[l][None, :],
                          rows=_tile_rows(seq, 512))
        blk = _tile_rows(seq, 512)
        qt, k, vt, gate = _mla_prep_call(xb, w_b, mla_q_norm[l][None, :], mla_kv_norm[l][None, :],
                                         wqa, wqb, wk, wv, ctab, stab, ctab_t, stab_t,
                                         rows=blk, bk=blk // 2)
        y_mla = _mla_attn_call(qt, k, vt, gate, bq=blk)
        y_mem = _mem_attn_call(xb, w_m, memb, w_mem_k[l].astype(BF16), w_mem_v[l].astype(BF16),
                               rows=_tile_rows(seq, 512))
        x, xb = _out_ln_call(y_hg, y_mla, y_mem, x, w_out[l].astype(BF16),
                             ln_g[l][None, :], ln_b[l][None, :],
                             rows=_tile_rows(seq, 512), alpha=alpha)
    return x
```

```python
import functools
import math

import jax
import jax.numpy as jnp
from jax import lax
from jax.experimental import pallas as pl
from jax.experimental.pallas import tpu as pltpu

F32 = jnp.float32
BF16 = jnp.bfloat16

LANES = 128

HG_HEADS = 8
HG_DIM = 128
HG_W = HG_HEADS * HG_DIM
HG_CHUNK = 64
HG_HALF = HG_CHUNK // 2
HG_SAFE_DECAY = 60.0

MLA_HEADS = 8
MLA_NOPE = 64
MLA_ROPE = 32
MLA_QK = MLA_NOPE + MLA_ROPE
MLA_VDIM = 64
MLA_Q_RANK = 256
MLA_KV_RANK = 128
MLA_PAD = LANES
MLA_W = MLA_HEADS * MLA_VDIM
MLA_VROWS = MLA_VDIM + 16
ATTN_STEP_SUBS = (4, 2)

MEM_HEADS = 4
MEM_DIM = 128
MEM_W = MEM_HEADS * MEM_DIM

RMS_EPS = 1e-6
LN_EPS = 1e-5
NEG_BIG = -1e30
LOG2_E = math.log2(math.e)

VMEM_LIMIT = 56 * 1024 * 1024


def _nt_dot(a, b):
    return lax.dot_general(a, b, (((1,), (1,)), ((), ())), preferred_element_type=F32)


def _dot(a, b):
    return jnp.dot(a, b, preferred_element_type=F32)


def _silu(x):
    return x * (1.0 / (1.0 + jnp.exp(-x)))


def _hgrn_body(x_ref, xn_ref, w_ref, lb_ref, g_ref, y_ref, *scratch, rows):
    half = rows // 2
    n_chunk = half // HG_CHUNK
    sets = (scratch[0:5], scratch[5:10])
    st_ref = scratch[10]

    @pl.when(pl.program_id(1) == 0)
    def _reset_state():
        st_ref[...] = jnp.zeros_like(st_ref)

    r_i = lax.broadcasted_iota(jnp.int32, (HG_CHUNK, HG_DIM), 0)
    c_i = lax.broadcasted_iota(jnp.int32, (HG_CHUNK, HG_DIM), 1)
    causal = r_i >= c_i
    zeros_chunk = jnp.zeros((HG_CHUNK, HG_DIM), BF16)
    zeros_v = jnp.zeros((HG_CHUNK, HG_DIM), F32)
    gain = g_ref[...]

    piece = 2 * HG_W // n_chunk

    def project(hf, c0, c1, nxt=False):
        x = xn_ref[0] if nxt else x_ref[0, hf * half:(hf + 1) * half, :]
        sets[hf][0][:, c0:c1] = _dot(x, w_ref[:, c0:c1])

    def gates(hf):
        h_ref, q_ref, k_ref = sets[hf][0:3]
        for h in range(HG_HEADS):
            cols = slice(HG_W + h * HG_DIM, HG_W + (h + 1) * HG_DIM)
            z = h_ref[:, cols]
            lb = lb_ref[:, h * HG_DIM:(h + 1) * HG_DIM]
            e = jnp.exp(-z)
            u = 1.0 + e
            h_ref[:, cols] = jnp.log(1.0 + lb * e) - jnp.log(u)
            k_ref[h] = (1.0 - lb) * (1.0 - 1.0 / u)
            q_ref[h] = h_ref[:, h * HG_DIM:(h + 1) * HG_DIM]

    def cumulative_decay(hf):
        h_ref, b_ref = sets[hf][0], sets[hf][3]
        row = lax.broadcasted_iota(jnp.int32, (8, HG_W), 0)
        worst = jnp.zeros((1, HG_W), F32)
        for j in range(n_chunk):
            total = jnp.zeros((1, HG_W), F32)
            marks = []
            for t in range(HG_CHUNK // 8):
                rws = slice(j * HG_CHUNK + 8 * t, j * HG_CHUNK + 8 * (t + 1))
                acc = h_ref[rws, HG_W:2 * HG_W]
                for shift in (1, 2, 4):
                    acc = acc + jnp.where(row >= shift, pltpu.roll(acc, shift, axis=0), 0.0)
                acc = acc + total
                total = acc[7:8]
                for h in range(HG_HEADS):
                    b_ref[h, rws, :] = acc[:, h * HG_DIM:(h + 1) * HG_DIM]
                if 8 * (t + 1) in (HG_HALF, HG_CHUNK):
                    marks.append(total)
            worst = jnp.maximum(worst, jnp.maximum(-marks[0], marks[0] - marks[1]))
        return jnp.max(worst) > HG_SAFE_DECAY

    def scores(hf, j):
        _, q_ref, k_ref, b_ref, a_ref = sets[hf]
        rws = slice(j * HG_CHUNK, (j + 1) * HG_CHUNK)
        for h in range(HG_HEADS):
            bc = b_ref[h, rws, :]
            rel = bc - bc[HG_HALF - 1:HG_HALF]
            q_mid = (q_ref[h, rws, :] * jnp.exp(rel)).astype(BF16)
            k_mid = (k_ref[h, rws, :] * jnp.exp(-rel)).astype(BF16)
            k_pad = jnp.concatenate([k_mid, zeros_chunk], axis=0)
            a_ref[j * HG_HEADS + h] = jnp.where(causal, _nt_dot(q_mid, k_pad), 0.0)

    def direct_scores(hf):
        _, q_ref, k_ref, b_ref, a_ref = sets[hf]

        def per_chunk_head(idx, carry):
            j = idx // HG_HEADS
            h = idx - j * HG_HEADS
            base = pl.multiple_of(j * HG_CHUNK, HG_CHUNK)
            bc = b_ref[h, pl.ds(base, HG_CHUNK), :]
            qc = q_ref[h, pl.ds(base, HG_CHUNK), :]

            def per_key(s, a):
                b_s = b_ref[h, pl.ds(base + s, 1), :]
                k_s = k_ref[h, pl.ds(base + s, 1), :]
                prod = qc * jnp.exp(jnp.minimum(bc - b_s, 0.0)) * k_s
                col = jnp.sum(prod, axis=-1, keepdims=True)
                return jnp.where((c_i == s) & (r_i >= s), col, a)

            a_ref[idx] = lax.fori_loop(0, HG_CHUNK, per_key,
                                       jnp.zeros((HG_CHUNK, HG_DIM), F32))
            return carry

        lax.fori_loop(0, n_chunk * HG_HEADS, per_chunk_head, 0)

    def readout(hf, j):
        h_ref, q_ref, k_ref, b_ref, a_ref = sets[hf]
        rws = slice(j * HG_CHUNK, (j + 1) * HG_CHUNK)
        out_rows = slice(hf * half + j * HG_CHUNK, hf * half + (j + 1) * HG_CHUNK)
        for h in range(HG_HEADS):
            hc = slice(h * HG_DIM, (h + 1) * HG_DIM)
            bc = b_ref[h, rws, :]
            qc = q_ref[h, rws, :]
            kc = k_ref[h, rws, :]
            vc = h_ref[rws, 2 * HG_W + h * HG_DIM:2 * HG_W + (h + 1) * HG_DIM]
            gate = h_ref[rws, 3 * HG_W + h * HG_DIM:3 * HG_W + (h + 1) * HG_DIM]
            b_last = bc[HG_CHUNK - 1:HG_CHUNK]
            st = st_ref[h]
            v_t = jnp.concatenate([vc, zeros_v], axis=0).T.astype(BF16)
            q_hat = (qc * jnp.exp(bc)).astype(BF16)
            k_dec = jnp.concatenate([(kc * jnp.exp(b_last - bc)).astype(BF16), zeros_chunk],
                                    axis=0)
            lhs = jnp.concatenate([a_ref[j * HG_HEADS + h].astype(BF16), q_hat], axis=1)
            rhs = jnp.concatenate([v_t, st.astype(BF16)], axis=1)
            o = _nt_dot(lhs, rhs)
            st_ref[h] = st * jnp.exp(b_last) + _dot(v_t, k_dec)
            o = o * lax.rsqrt(jnp.mean(o * o, axis=-1, keepdims=True) + RMS_EPS)
            y_ref[0, out_rows, hc] = (o * gain * _silu(gate)).astype(y_ref.dtype)

    @pl.when(pl.program_id(1) == 0)
    def _project_first_half():
        project(0, 0, 4 * HG_W)

    gates(0)
    unsafe_0 = cumulative_decay(0)
    for j in range(n_chunk):
        project(1, j * piece, (j + 1) * piece)
        scores(0, j)
    pl.when(unsafe_0)(functools.partial(direct_scores, 0))
    for j in range(n_chunk):
        project(1, 2 * HG_W + j * piece, 2 * HG_W + (j + 1) * piece)
        readout(0, j)
    gates(1)
    unsafe_1 = cumulative_decay(1)
    for j in range(n_chunk):
        project(0, j * piece, (j + 1) * piece, nxt=True)
        scores(1, j)
    pl.when(unsafe_1)(functools.partial(direct_scores, 1))
    for j in range(n_chunk):
        project(0, 2 * HG_W + j * piece, 2 * HG_W + (j + 1) * piece, nxt=True)
        readout(1, j)


def _hgrn_call(xb, w_in_b, layer, lb, gain, *, rows):
    bsz, seq, d = xb.shape
    body = functools.partial(_hgrn_body, rows=rows)
    half = rows // 2
    n_chunk = half // HG_CHUNK
    last_half = seq // half - 2
    half_set = [
        pltpu.VMEM((half, 4 * HG_W), F32),
        pltpu.VMEM((HG_HEADS, half, HG_DIM), F32),
        pltpu.VMEM((HG_HEADS, half, HG_DIM), F32),
        pltpu.VMEM((HG_HEADS, half, HG_DIM), F32),
        pltpu.VMEM((n_chunk * HG_HEADS, HG_CHUNK, HG_DIM), F32),
    ]
    return pl.pallas_call(
        body,
        grid=(bsz, seq // rows),
        in_specs=[
            pl.BlockSpec((1, rows, d), lambda b, t: (b, t, 0)),
            pl.BlockSpec((1, half, d), lambda b, t: (b, jnp.minimum(2 * t + 2, last_half), 0)),
            pl.BlockSpec((None, d, 4 * HG_W), lambda b, t: (layer, 0, 0)),
            pl.BlockSpec((1, HG_W), lambda b, t: (0, 0)),
            pl.BlockSpec((1, HG_DIM), lambda b, t: (0, 0)),
        ],
        out_specs=pl.BlockSpec((1, rows, HG_W), lambda b, t: (b, t, 0)),
        out_shape=jax.ShapeDtypeStruct((bsz, seq, HG_W), BF16),
        scratch_shapes=half_set + half_set + [pltpu.VMEM((HG_HEADS, HG_DIM, HG_DIM), F32)],
        compiler_params=pltpu.CompilerParams(
            dimension_semantics=("arbitrary", "arbitrary"), vmem_limit_bytes=VMEM_LIMIT),
        name="hgrn",
    )(xb, xb, w_in_b, lb, gain)


def _mla_prep_body(x_ref, w_ref, qn_ref, kvn_ref, wqa_ref, wqb_ref, wk_ref, wv_ref,
                   ct_ref, st_ref, ctt_ref, stt_ref, qt_ref, k_ref, vt_ref, g_ref):
    x = x_ref[0]
    h = _dot(x, w_ref[...])
    cq = h[:, :MLA_Q_RANK]
    ckv = h[:, MLA_Q_RANK:MLA_Q_RANK + MLA_KV_RANK]
    o = MLA_Q_RANK + MLA_KV_RANK
    kr = h[:, o:o + MLA_PAD]
    kr_rot = h[:, o + MLA_PAD:o + 2 * MLA_PAD]
    gate = h[:, o + 2 * MLA_PAD:]

    cqn = (cq * lax.rsqrt(jnp.mean(cq * cq, axis=-1, keepdims=True) + RMS_EPS)
           * qn_ref[...]).astype(BF16)
    c = (ckv * lax.rsqrt(jnp.mean(ckv * ckv, axis=-1, keepdims=True) + RMS_EPS)
         * kvn_ref[...]).astype(BF16)

    scale = LOG2_E / math.sqrt(MLA_QK)
    qa_t = _nt_dot(wqa_ref[...], cqn)
    qb_t = _nt_dot(wqb_ref[...], cqn)
    cq_tab = ctt_ref[0] * scale
    sq_tab = stt_ref[0] * scale
    for hd in range(MLA_HEADS):
        base = hd * MLA_PAD
        rope = slice(base + MLA_NOPE, base + MLA_QK)
        qt_ref[0, base:base + MLA_NOPE, :] = (
            qa_t[base:base + MLA_NOPE] * scale).astype(qt_ref.dtype)
        qt_ref[0, rope, :] = (qa_t[rope] * cq_tab
                              + qb_t[hd * MLA_ROPE:(hd + 1) * MLA_ROPE] * sq_tab).astype(qt_ref.dtype)
        qt_ref[0, base + MLA_QK:base + MLA_PAD, :] = jnp.zeros(
            (MLA_PAD - MLA_QK, qt_ref.shape[-1]), qt_ref.dtype)

    ct128 = jnp.concatenate([ct_ref[0]] * (MLA_PAD // MLA_ROPE), axis=1)
    st128 = jnp.concatenate([st_ref[0]] * (MLA_PAD // MLA_ROPE), axis=1)
    kr_roped = (kr * ct128 + kr_rot * st128).astype(BF16)
    ck = jnp.concatenate([c, kr_roped], axis=1)
    k_ref[0] = _dot(ck, wk_ref[...]).astype(k_ref.dtype)

    vt = _nt_dot(wv_ref[...], c).astype(vt_ref.dtype)
    bk = vt_ref.shape[-1]
    for hd in range(MLA_HEADS):
        for sub in range(vt_ref.shape[2]):
            vt_ref[0, hd, sub, :MLA_VDIM, :] = vt[hd * MLA_VDIM:(hd + 1) * MLA_VDIM,
                                                  sub * bk:(sub + 1) * bk]
            vt_ref[0, hd, sub, MLA_VDIM:, :] = jnp.ones((MLA_VROWS - MLA_VDIM, bk), vt_ref.dtype)
    g_ref[0] = _silu(gate).astype(g_ref.dtype)


def _mla_prep_call(xb, w_b, qn, kvn, wqa, wqb, wk, wv, ctab, stab, ctab_t, stab_t, *, rows, bk):
    bsz, seq, d = xb.shape
    hp = MLA_HEADS * MLA_PAD
    const = lambda b, t: (0, 0)
    return pl.pallas_call(
        _mla_prep_body,
        grid=(bsz, seq // rows),
        in_specs=[
            pl.BlockSpec((1, rows, d), lambda b, t: (b, t, 0)),
            pl.BlockSpec(w_b.shape, const),
            pl.BlockSpec((1, MLA_Q_RANK), const),
            pl.BlockSpec((1, MLA_KV_RANK), const),
            pl.BlockSpec(wqa.shape, const),
            pl.BlockSpec(wqb.shape, const),
            pl.BlockSpec(wk.shape, const),
            pl.BlockSpec(wv.shape, const),
            pl.BlockSpec((1, rows, MLA_ROPE), lambda b, t: (b, t, 0)),
            pl.BlockSpec((1, rows, MLA_ROPE), lambda b, t: (b, t, 0)),
            pl.BlockSpec((1, MLA_ROPE, rows), lambda b, t: (b, 0, t)),
            pl.BlockSpec((1, MLA_ROPE, rows), lambda b, t: (b, 0, t)),
        ],
        out_specs=[
            pl.BlockSpec((1, hp, rows), lambda b, t: (b, 0, t)),
            pl.BlockSpec((1, rows, hp), lambda b, t: (b, t, 0)),
            pl.BlockSpec((1, MLA_HEADS, rows // bk, MLA_VROWS, bk), lambda b, t: (b, 0, t, 0, 0)),
            pl.BlockSpec((1, rows, MLA_W), lambda b, t: (b, t, 0)),
        ],
        out_shape=[
            jax.ShapeDtypeStruct((bsz, hp, seq), BF16),
            jax.ShapeDtypeStruct((bsz, seq, hp), BF16),
            jax.ShapeDtypeStruct((bsz, MLA_HEADS, seq // bk, MLA_VROWS, bk), BF16),
            jax.ShapeDtypeStruct((bsz, seq, MLA_W), BF16),
        ],
        compiler_params=pltpu.CompilerParams(
            dimension_semantics=("arbitrary", "arbitrary"), vmem_limit_bytes=VMEM_LIMIT),
        name="mla_prep",
    )(xb, w_b, qn, kvn, wqa, wqb, wk, wv, ctab, stab, ctab_t, stab_t)


def _mla_attn_body(qt_ref, k_ref, vt_ref, g_ref, y_ref, *, bq, bk):
    qi = pl.program_id(2)
    n_diag = bq // bk
    assert ATTN_STEP_SUBS[-1] == n_diag and all(w % n_diag == 0 for w in ATTN_STEP_SUBS)
    key_i = lax.broadcasted_iota(jnp.int32, (bk, bq), 0)
    qry_i = lax.broadcasted_iota(jnp.int32, (bk, bq), 1)
    q_t = [qt_ref[0, hh * MLA_PAD:(hh + 1) * MLA_PAD, :] for hh in range(2)]

    def step(first_blk, carry, n_sub, masked):
        carry = list(carry)
        scores = []
        for sub in range(n_sub):
            start = pl.multiple_of((first_blk + sub) * bk, bk)
            scores.append([_dot(k_ref[0, pl.ds(start, bk), hh * MLA_PAD:(hh + 1) * MLA_PAD],
                                q_t[hh]) for hh in range(2)])
        for sub in range(n_sub):
            for hh in range(2):
                m, acc = carry[hh]
                s = scores[sub][hh]
                if masked:
                    s = jnp.where(key_i + sub * bk <= qry_i, s, NEG_BIG)
                m_new = jnp.maximum(m, jnp.max(s, axis=0, keepdims=True))
                p = jnp.exp2(s - m_new).astype(BF16)
                alpha = jnp.exp2(m - m_new)
                acc = alpha * acc + _dot(vt_ref[0, hh, first_blk + sub], p)
                carry[hh] = (m_new, acc)
        return tuple(carry)

    init = tuple((jnp.full((1, bq), NEG_BIG, F32), jnp.zeros((MLA_VROWS, bq), F32))
                 for _ in range(2))
    n_full = qi * n_diag
    carry, done = init, 0
    for width in ATTN_STEP_SUBS:
        n_steps = (n_full - done) // width
        carry = lax.fori_loop(
            0, n_steps,
            lambda t, c, width=width, done=done: step(done + t * width, c, width, False), carry)
        done = done + n_steps * width
    carry = step(n_full, carry, n_diag, True)
    o_t = jnp.concatenate([acc[:MLA_VDIM] * (1.0 / acc[MLA_VDIM:MLA_VDIM + 1])
                           for (_, acc) in carry], axis=0)
    y_ref[0] = (o_t.T * g_ref[0].astype(F32)).astype(y_ref.dtype)


def _mla_attn_call(qt, k, vt, gate, *, bq):
    bsz, seq, hp = k.shape
    bk = vt.shape[-1]
    body = functools.partial(_mla_attn_body, bq=bq, bk=bk)
    return pl.pallas_call(
        body,
        grid=(bsz, MLA_HEADS // 2, seq // bq),
        in_specs=[
            pl.BlockSpec((1, 2 * MLA_PAD, bq), lambda b, h, i: (b, h, i)),
            pl.BlockSpec((1, seq, 2 * MLA_PAD), lambda b, h, i: (b, 0, h)),
            pl.BlockSpec((1, 2, seq // bk, MLA_VROWS, bk), lambda b, h, i: (b, h, 0, 0, 0)),
            pl.BlockSpec((1, bq, 2 * MLA_VDIM), lambda b, h, i: (b, i, h)),
        ],
        out_specs=pl.BlockSpec((1, bq, 2 * MLA_VDIM), lambda b, h, i: (b, i, h)),
        out_shape=jax.ShapeDtypeStruct((bsz, seq, MLA_W), BF16),
        compiler_params=pltpu.CompilerParams(
            dimension_semantics=("arbitrary", "arbitrary", "arbitrary"),
            vmem_limit_bytes=VMEM_LIMIT),
        name="mla_attn",
    )(qt, k, vt, gate)


def _mem_attn_body(x_ref, w_ref, mem_ref, wk_ref, wv_ref, y_ref, *rest):
    k_ref, v_ref = rest[-2:]

    @pl.when(pl.program_id(1) == 0)
    def _project_memory():
        mem = mem_ref[0].astype(BF16)
        k_ref[...] = _dot(mem, wk_ref[...]).astype(k_ref.dtype)
        v_ref[...] = _dot(mem, wv_ref[...]).astype(v_ref.dtype)

    x = x_ref[0].astype(BF16)
    if len(rest) == 3:
        rest[0][0] = x
    scale = 1.0 / math.sqrt(MEM_DIM)
    q_all = _dot(x, w_ref[:, :MEM_W])
    heads = [slice(hd * MEM_DIM, (hd + 1) * MEM_DIM) for hd in range(MEM_HEADS)]
    scores = [_nt_dot((q_all[:, cols] * scale).astype(BF16), k_ref[:, cols]) for cols in heads]
    gate = _dot(x, w_ref[:, MEM_W:])
    for cols, s in zip(heads, scores):
        p = jnp.exp(s - jnp.max(s, axis=-1, keepdims=True))
        inv = 1.0 / jnp.sum(p, axis=-1, keepdims=True)
        o = _dot(p.astype(BF16), v_ref[:, cols]) * inv
        y_ref[0, :, cols] = (o * _silu(gate[:, cols])).astype(y_ref.dtype)


def _mem_attn_call(x, w_m, mem, wk, wv, *, rows):
    bsz, seq, d = x.shape
    mem_len = mem.shape[1]
    const = lambda b, t: (0, 0)
    row_blk = lambda width: pl.BlockSpec((1, rows, width), lambda b, t: (b, t, 0))
    emit_cast = x.dtype != BF16
    out_specs = [row_blk(MEM_W)] + ([row_blk(d)] if emit_cast else [])
    out_shape = [jax.ShapeDtypeStruct((bsz, seq, MEM_W), BF16)] + (
        [jax.ShapeDtypeStruct((bsz, seq, d), BF16)] if emit_cast else [])
    outs = pl.pallas_call(
        _mem_attn_body,
        grid=(bsz, seq // rows),
        in_specs=[
            pl.BlockSpec((1, rows, d), lambda b, t: (b, t, 0)),
            pl.BlockSpec(w_m.shape, const),
            pl.BlockSpec((1, mem_len, d), lambda b, t: (b, 0, 0)),
            pl.BlockSpec(wk.shape, const),
            pl.BlockSpec(wv.shape, const),
        ],
        out_specs=out_specs,
        out_shape=out_shape,
        scratch_shapes=[pltpu.VMEM((mem_len, MEM_W), BF16), pltpu.VMEM((mem_len, MEM_W), BF16)],
        compiler_params=pltpu.CompilerParams(
            dimension_semantics=("arbitrary", "arbitrary"), vmem_limit_bytes=VMEM_LIMIT),
        name="mem_attn",
    )(x, w_m, mem, wk, wv)
    return (outs[0], outs[1]) if emit_cast else (outs[0], x)


def _out_ln_body(yh_ref, ya_ref, ym_ref, x_ref, w_ref, g_ref, b_ref, o_ref, ob_ref, *, alpha):
    w = w_ref
    half = x_ref.shape[1] // 2
    halves = [slice(r * half, (r + 1) * half) for r in range(2)]
    ys = [_dot(yh_ref[0, rs], w[:HG_W]) + _dot(ya_ref[0, rs], w[HG_W:HG_W + MLA_W])
          + _dot(ym_ref[0, rs], w[HG_W + MLA_W:]) for rs in halves]
    for rs, y in zip(halves, ys):
        z = alpha * x_ref[0, rs] + y
        mu = jnp.mean(z, axis=-1, keepdims=True)
        zc = z - mu
        var = jnp.mean(zc * zc, axis=-1, keepdims=True)
        out = zc * lax.rsqrt(var + LN_EPS) * g_ref[...] + b_ref[...]
        o_ref[0, rs] = out
        ob_ref[0, rs] = out.astype(ob_ref.dtype)


def _out_ln_call(y_hg, y_mla, y_mem, x, w_out, g, b, *, rows, alpha):
    bsz, seq, d = x.shape
    const = lambda b_, t: (0, 0)
    body = functools.partial(_out_ln_body, alpha=alpha)
    row_blk = lambda width: pl.BlockSpec((1, rows, width), lambda b_, t: (b_, t, 0))
    return pl.pallas_call(
        body,
        grid=(bsz, seq // rows),
        in_specs=[row_blk(HG_W), row_blk(MLA_W), row_blk(MEM_W), row_blk(d),
                  pl.BlockSpec(w_out.shape, const),
                  pl.BlockSpec((1, d), const), pl.BlockSpec((1, d), const)],
        out_specs=[row_blk(d), row_blk(d)],
        out_shape=[jax.ShapeDtypeStruct((bsz, seq, d), F32),
                   jax.ShapeDtypeStruct((bsz, seq, d), BF16)],
        compiler_params=pltpu.CompilerParams(
            dimension_semantics=("arbitrary", "arbitrary"), vmem_limit_bytes=VMEM_LIMIT),
        name="out_ln",
    )(y_hg, y_mla, y_mem, x, w_out, g, b)


def _rotate_half_cols(w):
    half = w.shape[-1] // 2
    return jnp.concatenate([-w[..., half:], w[..., :half]], axis=-1)


def _pad_heads(w, heads, width):
    k = w.shape[0]
    w = w.reshape(k, heads, width)
    w = jnp.pad(w, ((0, 0), (0, 0), (0, MLA_PAD - width)))
    return w.reshape(k, heads * MLA_PAD)


def _mla_weights(w_in_l, w_uq, w_uk, w_uv):
    o = 4 * HG_W
    w_cq = w_in_l[:, o:o + MLA_Q_RANK]
    w_ckv = w_in_l[:, o + MLA_Q_RANK:o + MLA_Q_RANK + MLA_KV_RANK]
    o2 = o + MLA_Q_RANK + MLA_KV_RANK
    w_kr = w_in_l[:, o2:o2 + MLA_ROPE]
    w_gate = w_in_l[:, o2 + MLA_ROPE:o2 + MLA_ROPE + MLA_W]
    lane_pad = lambda w: jnp.pad(w, ((0, 0), (MLA_NOPE, MLA_PAD - MLA_QK)))
    w_b = jnp.concatenate([w_cq, w_ckv, lane_pad(w_kr), lane_pad(_rotate_half_cols(w_kr)), w_gate],
                          axis=1).astype(BF16)

    uq = w_uq.reshape(MLA_Q_RANK, MLA_HEADS, MLA_QK)
    uq_rot = jnp.concatenate([jnp.zeros_like(uq[..., :MLA_NOPE]),
                              _rotate_half_cols(uq[..., MLA_NOPE:])], axis=-1)
    wqa = _pad_heads(uq.reshape(MLA_Q_RANK, -1), MLA_HEADS, MLA_QK).T.astype(BF16)
    wqb = uq_rot[..., MLA_NOPE:].reshape(MLA_Q_RANK, -1).T.astype(BF16)

    uk = w_uk.reshape(MLA_KV_RANK, MLA_HEADS, MLA_NOPE)
    top = jnp.pad(uk, ((0, 0), (0, 0), (0, MLA_PAD - MLA_NOPE)))
    pick = jnp.eye(MLA_PAD, dtype=uk.dtype)
    pick = pick * ((jnp.arange(MLA_PAD) >= MLA_NOPE) & (jnp.arange(MLA_PAD) < MLA_QK))
    bot = jnp.broadcast_to(pick[:, None, :], (MLA_PAD, MLA_HEADS, MLA_PAD))
    wk = jnp.concatenate([top, bot], axis=0).reshape(MLA_KV_RANK + MLA_PAD, -1).astype(BF16)

    wv = w_uv.T.astype(BF16)
    return w_b, wqa, wqb, wk, wv


def _rope_tables(positions):
    inv_freq = 1.0 / (10000.0 ** (jnp.arange(0, MLA_ROPE, 2, dtype=F32) / MLA_ROPE))
    ang = positions.astype(F32)[..., None] * inv_freq
    cos, sin = jnp.cos(ang), jnp.sin(ang)
    return jnp.concatenate([cos, cos], axis=-1), jnp.concatenate([sin, sin], axis=-1)


def _tile_rows(seq, want):
    rows = min(seq, want)
    assert seq % rows == 0
    return rows


def kernel(x, mem, positions, w_in, hgrn_lb_logits, hgrn_out_norm, mla_q_norm, mla_kv_norm,
           w_mla_uq, w_mla_uk, w_mla_uv, w_mem_k, w_mem_v, w_out, ln_g, ln_b):
    depth = w_in.shape[0]
    seq = x.shape[1]
    alpha = (2 * depth) ** 0.25
    ctab, stab = _rope_tables(positions)
    ctab_t, stab_t = jnp.swapaxes(ctab, 1, 2), jnp.swapaxes(stab, 1, 2)
    lb_soft = jax.nn.softmax(hgrn_lb_logits.astype(F32), axis=0)
    lower_bounds = jnp.cumsum(lb_soft, axis=0) - lb_soft[0]
    o_mem = 4 * HG_W + MLA_Q_RANK + MLA_KV_RANK + MLA_ROPE + MLA_W

    w_in_b = w_in.astype(BF16)
    xb = x
    for l in range(depth):
        w_l = w_in_b[l]
        w_b, wqa, wqb, wk, wv = _mla_weights(w_l, w_mla_uq[l], w_mla_uk[l], w_mla_uv[l])
        w_m = w_l[:, o_mem:]

        y_mem, xb = _mem_attn_call(xb, w_m, mem, w_mem_k[l].astype(BF16),
                                   w_mem_v[l].astype(BF16), rows=_tile_rows(seq, 512))
        y_hg = _hgrn_call(xb, w_in_b, l, lower_bounds[l][None, :], hgrn_out_norm[l][None, :],
                          rows=_tile_rows(seq, 512))
        blk = _tile_rows(seq, 512)
        qt, k, vt, gate = _mla_prep_call(xb, w_b, mla_q_norm[l][None, :], mla_kv_norm[l][None, :],
                                         wqa, wqb, wk, wv, ctab, stab, ctab_t, stab_t,
                                         rows=blk, bk=blk // 2)
        y_mla = _mla_attn_call(qt, k, vt, gate, bq=blk)
        x, xb = _out_ln_call(y_hg, y_mla, y_mem, x, w_out[l].astype(BF16),
                             ln_g[l][None, :], ln_b[l][None, :],
                             rows=_tile_rows(seq, 512), alpha=alpha)
    return x
```

```python
import functools
import math

import jax
import jax.numpy as jnp
from jax import lax
from jax.experimental import pallas as pl
from jax.experimental.pallas import tpu as pltpu

F32 = jnp.float32
BF16 = jnp.bfloat16

LANES = 128

HG_HEADS = 8
HG_DIM = 128
HG_W = HG_HEADS * HG_DIM
HG_CHUNK = 64
HG_HALF = HG_CHUNK // 2
HG_SAFE_DECAY = 60.0

MLA_HEADS = 8
MLA_NOPE = 64
MLA_ROPE = 32
MLA_QK = MLA_NOPE + MLA_ROPE
MLA_VDIM = 64
MLA_Q_RANK = 256
MLA_KV_RANK = 128
MLA_PAD = LANES
MLA_W = MLA_HEADS * MLA_VDIM
MLA_VROWS = MLA_VDIM + 16
ATTN_STEP_SUBS = (8, 4, 2)

MEM_HEADS = 4
MEM_DIM = 128
MEM_W = MEM_HEADS * MEM_DIM

RMS_EPS = 1e-6
LN_EPS = 1e-5
NEG_BIG = -1e30
LOG2_E = math.log2(math.e)

V7X_VMEM_BYTES = 64 * 1024 * 1024
VMEM_LIMIT = V7X_VMEM_BYTES - 8 * 1024 * 1024
ROW_TILE = 512


def _nt_dot(a, b):
    return lax.dot_general(a, b, (((1,), (1,)), ((), ())), preferred_element_type=F32)


def _dot(a, b):
    return jnp.dot(a, b, preferred_element_type=F32)


def _silu(x):
    return x * (1.0 / (1.0 + jnp.exp(-x)))


def _hgrn_body(x_ref, xn_ref, w_ref, lb_ref, g_ref, y_ref, *scratch, rows):
    half = rows // 2
    n_chunk = half // HG_CHUNK
    sets = (scratch[0:5], scratch[5:10])
    st_ref = scratch[10]

    @pl.when(pl.program_id(1) == 0)
    def _reset_state():
        st_ref[...] = jnp.zeros_like(st_ref)

    r_i = lax.broadcasted_iota(jnp.int32, (HG_CHUNK, HG_DIM), 0)
    c_i = lax.broadcasted_iota(jnp.int32, (HG_CHUNK, HG_DIM), 1)
    causal = r_i >= c_i
    zeros_chunk = jnp.zeros((HG_CHUNK, HG_DIM), BF16)
    zeros_v = jnp.zeros((HG_CHUNK, HG_DIM), F32)
    gain = g_ref[...]

    piece = 2 * HG_W // n_chunk

    def project(hf, c0, c1, nxt=False):
        x = xn_ref[0] if nxt else x_ref[0, hf * half:(hf + 1) * half, :]
        sets[hf][0][:, c0:c1] = _dot(x, w_ref[:, c0:c1])

    def gates(hf):
        h_ref, q_ref, k_ref = sets[hf][0:3]
        for h in range(HG_HEADS):
            cols = slice(HG_W + h * HG_DIM, HG_W + (h + 1) * HG_DIM)
            z = h_ref[:, cols]
            lb = lb_ref[:, h * HG_DIM:(h + 1) * HG_DIM]
            e = jnp.exp(-z)
            u = 1.0 + e
            h_ref[:, cols] = jnp.log(1.0 + lb * e) - jnp.log(u)
            k_ref[h] = (1.0 - lb) * (1.0 - 1.0 / u)
            q_ref[h] = h_ref[:, h * HG_DIM:(h + 1) * HG_DIM]

    def cumulative_decay(hf):
        h_ref, b_ref = sets[hf][0], sets[hf][3]
        row = lax.broadcasted_iota(jnp.int32, (8, HG_W), 0)
        worst = jnp.zeros((1, HG_W), F32)
        for j in range(n_chunk):
            total = jnp.zeros((1, HG_W), F32)
            marks = []
            for t in range(HG_CHUNK // 8):
                rws = slice(j * HG_CHUNK + 8 * t, j * HG_CHUNK + 8 * (t + 1))
                acc = h_ref[rws, HG_W:2 * HG_W]
                for shift in (1, 2, 4):
                    acc = acc + jnp.where(row >= shift, pltpu.roll(acc, shift, axis=0), 0.0)
                acc = acc + total
                total = acc[7:8]
                for h in range(HG_HEADS):
                    b_ref[h, rws, :] = acc[:, h * HG_DIM:(h + 1) * HG_DIM]
                if 8 * (t + 1) in (HG_HALF, HG_CHUNK):
                    marks.append(total)
            worst = jnp.maximum(worst, jnp.maximum(-marks[0], marks[0] - marks[1]))
        return jnp.max(worst) > HG_SAFE_DECAY

    def scores(hf, j):
        _, q_ref, k_ref, b_ref, a_ref = sets[hf]
        rws = slice(j * HG_CHUNK, (j + 1) * HG_CHUNK)
        for h in range(HG_HEADS):
            bc = b_ref[h, rws, :]
            rel = bc - bc[HG_HALF - 1:HG_HALF]
            q_mid = (q_ref[h, rws, :] * jnp.exp(rel)).astype(BF16)
            k_mid = (k_ref[h, rws, :] * jnp.exp(-rel)).astype(BF16)
            k_pad = jnp.concatenate([k_mid, zeros_chunk], axis=0)
            a_ref[j * HG_HEADS + h] = jnp.where(causal, _nt_dot(q_mid, k_pad), 0.0)

    def direct_scores(hf):
        _, q_ref, k_ref, b_ref, a_ref = sets[hf]

        def per_chunk_head(idx, carry):
            j = idx // HG_HEADS
            h = idx - j * HG_HEADS
            base = pl.multiple_of(j * HG_CHUNK, HG_CHUNK)
            bc = b_ref[h, pl.ds(base, HG_CHUNK), :]
            qc = q_ref[h, pl.ds(base, HG_CHUNK), :]

            def per_key(s, a):
                b_s = b_ref[h, pl.ds(base + s, 1), :]
                k_s = k_ref[h, pl.ds(base + s, 1), :]
                prod = qc * jnp.exp(jnp.minimum(bc - b_s, 0.0)) * k_s
                col = jnp.sum(prod, axis=-1, keepdims=True)
                return jnp.where((c_i == s) & (r_i >= s), col, a)

            a_ref[idx] = lax.fori_loop(0, HG_CHUNK, per_key,
                                       jnp.zeros((HG_CHUNK, HG_DIM), F32))
            return carry

        lax.fori_loop(0, n_chunk * HG_HEADS, per_chunk_head, 0)

    def readout(hf, j):
        h_ref, q_ref, k_ref, b_ref, a_ref = sets[hf]
        rws = slice(j * HG_CHUNK, (j + 1) * HG_CHUNK)
        out_rows = slice(hf * half + j * HG_CHUNK, hf * half + (j + 1) * HG_CHUNK)
        for h in range(HG_HEADS):
            hc = slice(h * HG_DIM, (h + 1) * HG_DIM)
            bc = b_ref[h, rws, :]
            qc = q_ref[h, rws, :]
            kc = k_ref[h, rws, :]
            vc = h_ref[rws, 2 * HG_W + h * HG_DIM:2 * HG_W + (h + 1) * HG_DIM]
            gate = h_ref[rws, 3 * HG_W + h * HG_DIM:3 * HG_W + (h + 1) * HG_DIM]
            b_last = bc[HG_CHUNK - 1:HG_CHUNK]
            st = st_ref[h]
            v_t = jnp.concatenate([vc, zeros_v], axis=0).T.astype(BF16)
            q_hat = (qc * jnp.exp(bc)).astype(BF16)
            k_dec = jnp.concatenate([(kc * jnp.exp(b_last - bc)).astype(BF16), zeros_chunk],
                                    axis=0)
            lhs = jnp.concatenate([a_ref[j * HG_HEADS + h].astype(BF16), q_hat], axis=1)
            rhs = jnp.concatenate([v_t, st.astype(BF16)], axis=1)
            o = _nt_dot(lhs, rhs)
            st_ref[h] = st * jnp.exp(b_last) + _dot(v_t, k_dec)
            o = o * lax.rsqrt(jnp.mean(o * o, axis=-1, keepdims=True) + RMS_EPS)
            y_ref[0, out_rows, hc] = (o * gain * _silu(gate)).astype(y_ref.dtype)

    @pl.when(pl.program_id(1) == 0)
    def _project_first_half():
        project(0, 0, 4 * HG_W)

    gates(0)
    unsafe_0 = cumulative_decay(0)
    for j in range(n_chunk):
        project(1, j * piece, (j + 1) * piece)
        scores(0, j)
    pl.when(unsafe_0)(functools.partial(direct_scores, 0))
    for j in range(n_chunk):
        project(1, 2 * HG_W + j * piece, 2 * HG_W + (j + 1) * piece)
        readout(0, j)
    gates(1)
    unsafe_1 = cumulative_decay(1)
    for j in range(n_chunk):
        project(0, j * piece, (j + 1) * piece, nxt=True)
        scores(1, j)
    pl.when(unsafe_1)(functools.partial(direct_scores, 1))
    for j in range(n_chunk):
        project(0, 2 * HG_W + j * piece, 2 * HG_W + (j + 1) * piece, nxt=True)
        readout(1, j)


def _hgrn_call(xb, w_in_b, layer, lb, gain, *, rows):
    bsz, seq, d = xb.shape
    body = functools.partial(_hgrn_body, rows=rows)
    half = rows // 2
    n_chunk = half // HG_CHUNK
    last_half = seq // half - 2
    half_set = [
        pltpu.VMEM((half, 4 * HG_W), F32),
        pltpu.VMEM((HG_HEADS, half, HG_DIM), F32),
        pltpu.VMEM((HG_HEADS, half, HG_DIM), F32),
        pltpu.VMEM((HG_HEADS, half, HG_DIM), F32),
        pltpu.VMEM((n_chunk * HG_HEADS, HG_CHUNK, HG_DIM), F32),
    ]
    return pl.pallas_call(
        body,
        grid=(bsz, seq // rows),
        in_specs=[
            pl.BlockSpec((1, rows, d), lambda b, t: (b, t, 0)),
            pl.BlockSpec((1, half, d), lambda b, t: (b, jnp.minimum(2 * t + 2, last_half), 0)),
            pl.BlockSpec((None, d, 4 * HG_W), lambda b, t: (layer, 0, 0)),
            pl.BlockSpec((1, HG_W), lambda b, t: (0, 0)),
            pl.BlockSpec((1, HG_DIM), lambda b, t: (0, 0)),
        ],
        out_specs=pl.BlockSpec((1, rows, HG_W), lambda b, t: (b, t, 0)),
        out_shape=jax.ShapeDtypeStruct((bsz, seq, HG_W), BF16),
        scratch_shapes=half_set + half_set + [pltpu.VMEM((HG_HEADS, HG_DIM, HG_DIM), F32)],
        compiler_params=pltpu.CompilerParams(
            dimension_semantics=("arbitrary", "arbitrary"), vmem_limit_bytes=VMEM_LIMIT),
        name="hgrn",
    )(xb, xb, w_in_b, lb, gain)


def _mla_prep_body(x_ref, w_ref, qn_ref, kvn_ref, wqa_ref, wqb_ref, wk_ref, wv_ref,
                   ct_ref, st_ref, ctt_ref, stt_ref, qt_ref, k_ref, vt_ref, g_ref):
    x = x_ref[0]
    h = _dot(x, w_ref[...])
    cq = h[:, :MLA_Q_RANK]
    ckv = h[:, MLA_Q_RANK:MLA_Q_RANK + MLA_KV_RANK]
    o = MLA_Q_RANK + MLA_KV_RANK
    kr = h[:, o:o + MLA_PAD]
    kr_rot = h[:, o + MLA_PAD:o + 2 * MLA_PAD]
    gate = h[:, o + 2 * MLA_PAD:]

    cqn = (cq * lax.rsqrt(jnp.mean(cq * cq, axis=-1, keepdims=True) + RMS_EPS)
           * qn_ref[...]).astype(BF16)
    c = (ckv * lax.rsqrt(jnp.mean(ckv * ckv, axis=-1, keepdims=True) + RMS_EPS)
         * kvn_ref[...]).astype(BF16)

    scale = LOG2_E / math.sqrt(MLA_QK)
    qa_t = _nt_dot(wqa_ref[...], cqn)
    qb_t = _nt_dot(wqb_ref[...], cqn)
    cq_tab = ctt_ref[0] * scale
    sq_tab = stt_ref[0] * scale
    for hd in range(MLA_HEADS):
        base = hd * MLA_PAD
        rope = slice(base + MLA_NOPE, base + MLA_QK)
        qt_ref[0, base:base + MLA_NOPE, :] = (
            qa_t[base:base + MLA_NOPE] * scale).astype(qt_ref.dtype)
        qt_ref[0, rope, :] = (qa_t[rope] * cq_tab
                              + qb_t[hd * MLA_ROPE:(hd + 1) * MLA_ROPE] * sq_tab).astype(qt_ref.dtype)
        qt_ref[0, base + MLA_QK:base + MLA_PAD, :] = jnp.zeros(
            (MLA_PAD - MLA_QK, qt_ref.shape[-1]), qt_ref.dtype)

    ct128 = jnp.concatenate([ct_ref[0]] * (MLA_PAD // MLA_ROPE), axis=1)
    st128 = jnp.concatenate([st_ref[0]] * (MLA_PAD // MLA_ROPE), axis=1)
    kr_roped = (kr * ct128 + kr_rot * st128).astype(BF16)
    ck = jnp.concatenate([c, kr_roped], axis=1)
    k_ref[0] = _dot(ck, wk_ref[...]).astype(k_ref.dtype)

    vt = _nt_dot(wv_ref[...], c).astype(vt_ref.dtype)
    bk = vt_ref.shape[-1]
    for hd in range(MLA_HEADS):
        for sub in range(vt_ref.shape[2]):
            vt_ref[0, hd, sub, :MLA_VDIM, :] = vt[hd * MLA_VDIM:(hd + 1) * MLA_VDIM,
                                                  sub * bk:(sub + 1) * bk]
            vt_ref[0, hd, sub, MLA_VDIM:, :] = jnp.ones((MLA_VROWS - MLA_VDIM, bk), vt_ref.dtype)
    g_ref[0] = _silu(gate).astype(g_ref.dtype)


def _mla_prep_call(xb, w_b, qn, kvn, wqa, wqb, wk, wv, ctab, stab, ctab_t, stab_t, *, rows, bk):
    bsz, seq, d = xb.shape
    hp = MLA_HEADS * MLA_PAD
    const = lambda b, t: (0, 0)
    return pl.pallas_call(
        _mla_prep_body,
        grid=(bsz, seq // rows),
        in_specs=[
            pl.BlockSpec((1, rows, d), lambda b, t: (b, t, 0)),
            pl.BlockSpec(w_b.shape, const),
            pl.BlockSpec((1, MLA_Q_RANK), const),
            pl.BlockSpec((1, MLA_KV_RANK), const),
            pl.BlockSpec(wqa.shape, const),
            pl.BlockSpec(wqb.shape, const),
            pl.BlockSpec(wk.shape, const),
            pl.BlockSpec(wv.shape, const),
            pl.BlockSpec((1, rows, MLA_ROPE), lambda b, t: (b, t, 0)),
            pl.BlockSpec((1, rows, MLA_ROPE), lambda b, t: (b, t, 0)),
            pl.BlockSpec((1, MLA_ROPE, rows), lambda b, t: (b, 0, t)),
            pl.BlockSpec((1, MLA_ROPE, rows), lambda b, t: (b, 0, t)),
        ],
        out_specs=[
            pl.BlockSpec((1, hp, rows), lambda b, t: (b, 0, t)),
            pl.BlockSpec((1, rows, hp), lambda b, t: (b, t, 0)),
            pl.BlockSpec((1, MLA_HEADS, rows // bk, MLA_VROWS, bk), lambda b, t: (b, 0, t, 0, 0)),
            pl.BlockSpec((1, rows, MLA_W), lambda b, t: (b, t, 0)),
        ],
        out_shape=[
            jax.ShapeDtypeStruct((bsz, hp, seq), BF16),
            jax.ShapeDtypeStruct((bsz, seq, hp), BF16),
            jax.ShapeDtypeStruct((bsz, MLA_HEADS, seq // bk, MLA_VROWS, bk), BF16),
            jax.ShapeDtypeStruct((bsz, seq, MLA_W), BF16),
        ],
        compiler_params=pltpu.CompilerParams(
            dimension_semantics=("arbitrary", "arbitrary"), vmem_limit_bytes=VMEM_LIMIT),
        name="mla_prep",
    )(xb, w_b, qn, kvn, wqa, wqb, wk, wv, ctab, stab, ctab_t, stab_t)


def _mla_attn_body(qt_ref, k_ref, vt_ref, g_ref, y_ref, *, bq, bk):
    qi = pl.program_id(2)
    n_diag = bq // bk
    assert ATTN_STEP_SUBS[-1] == n_diag and all(w % n_diag == 0 for w in ATTN_STEP_SUBS)
    key_i = lax.broadcasted_iota(jnp.int32, (bk, bq), 0)
    qry_i = lax.broadcasted_iota(jnp.int32, (bk, bq), 1)
    q_t = [qt_ref[0, hh * MLA_PAD:(hh + 1) * MLA_PAD, :] for hh in range(2)]

    def step(first_blk, carry, n_sub, masked):
        carry = list(carry)
        scores = []
        for sub in range(n_sub):
            start = pl.multiple_of((first_blk + sub) * bk, bk)
            scores.append([_dot(k_ref[0, pl.ds(start, bk), hh * MLA_PAD:(hh + 1) * MLA_PAD],
                                q_t[hh]) for hh in range(2)])
        for sub in range(n_sub):
            for hh in range(2):
                m, acc = carry[hh]
                s = scores[sub][hh]
                if masked:
                    s = jnp.where(key_i + sub * bk <= qry_i, s, NEG_BIG)
                m_new = jnp.maximum(m, jnp.max(s, axis=0, keepdims=True))
                p = jnp.exp2(s - m_new).astype(BF16)
                alpha = jnp.exp2(m - m_new)
                acc = alpha * acc + _dot(vt_ref[0, hh, first_blk + sub], p)
                carry[hh] = (m_new, acc)
        return tuple(carry)

    init = tuple((jnp.full((1, bq), NEG_BIG, F32), jnp.zeros((MLA_VROWS, bq), F32))
                 for _ in range(2))
    n_full = qi * n_diag
    carry, done = init, 0
    for width in ATTN_STEP_SUBS:
        n_steps = (n_full - done) // width
        carry = lax.fori_loop(
            0, n_steps,
            lambda t, c, width=width, done=done: step(done + t * width, c, width, False), carry)
        done = done + n_steps * width
    carry = step(n_full, carry, n_diag, True)
    o_t = jnp.concatenate([acc[:MLA_VDIM] * (1.0 / acc[MLA_VDIM:MLA_VDIM + 1])
                           for (_, acc) in carry], axis=0)
    y_ref[0] = (o_t.T * g_ref[0].astype(F32)).astype(y_ref.dtype)


def _mla_attn_call(qt, k, vt, gate, *, bq):
    bsz, seq, hp = k.shape
    bk = vt.shape[-1]
    body = functools.partial(_mla_attn_body, bq=bq, bk=bk)
    return pl.pallas_call(
        body,
        grid=(bsz, MLA_HEADS // 2, seq // bq),
        in_specs=[
            pl.BlockSpec((1, 2 * MLA_PAD, bq), lambda b, h, i: (b, h, i)),
            pl.BlockSpec((1, seq, 2 * MLA_PAD), lambda b, h, i: (b, 0, h)),
            pl.BlockSpec((1, 2, seq // bk, MLA_VROWS, bk), lambda b, h, i: (b, h, 0, 0, 0)),
            pl.BlockSpec((1, bq, 2 * MLA_VDIM), lambda b, h, i: (b, i, h)),
        ],
        out_specs=pl.BlockSpec((1, bq, 2 * MLA_VDIM), lambda b, h, i: (b, i, h)),
        out_shape=jax.ShapeDtypeStruct((bsz, seq, MLA_W), BF16),
        compiler_params=pltpu.CompilerParams(
            dimension_semantics=("arbitrary", "arbitrary", "arbitrary"),
            vmem_limit_bytes=VMEM_LIMIT),
        name="mla_attn",
    )(qt, k, vt, gate)


def _mem_attn_body(x_ref, w_ref, mem_ref, wk_ref, wv_ref, y_ref, *rest):
    k_ref, v_ref = rest[-2:]

    @pl.when(pl.program_id(1) == 0)
    def _project_memory():
        mem = mem_ref[0].astype(BF16)
        k_ref[...] = _dot(mem, wk_ref[...]).astype(k_ref.dtype)
        v_ref[...] = _dot(mem, wv_ref[...]).astype(v_ref.dtype)

    x = x_ref[0].astype(BF16)
    if len(rest) == 3:
        rest[0][0] = x
    scale = 1.0 / math.sqrt(MEM_DIM)
    q_all = _dot(x, w_ref[:, :MEM_W])
    heads = [slice(hd * MEM_DIM, (hd + 1) * MEM_DIM) for hd in range(MEM_HEADS)]
    scores = [_nt_dot((q_all[:, cols] * scale).astype(BF16), k_ref[:, cols]) for cols in heads]
    gate = _dot(x, w_ref[:, MEM_W:])
    for cols, s in zip(heads, scores):
        p = jnp.exp(s - jnp.max(s, axis=-1, keepdims=True))
        inv = 1.0 / jnp.sum(p, axis=-1, keepdims=True)
        o = _dot(p.astype(BF16), v_ref[:, cols]) * inv
        y_ref[0, :, cols] = (o * _silu(gate[:, cols])).astype(y_ref.dtype)


def _mem_attn_call(x, w_m, mem, wk, wv, *, rows):
    bsz, seq, d = x.shape
    mem_len = mem.shape[1]
    const = lambda b, t: (0, 0)
    row_blk = lambda width: pl.BlockSpec((1, rows, width), lambda b, t: (b, t, 0))
    emit_cast = x.dtype != BF16
    out_specs = [row_blk(MEM_W)] + ([row_blk(d)] if emit_cast else [])
    out_shape = [jax.ShapeDtypeStruct((bsz, seq, MEM_W), BF16)] + (
        [jax.ShapeDtypeStruct((bsz, seq, d), BF16)] if emit_cast else [])
    outs = pl.pallas_call(
        _mem_attn_body,
        grid=(bsz, seq // rows),
        in_specs=[
            pl.BlockSpec((1, rows, d), lambda b, t: (b, t, 0)),
            pl.BlockSpec(w_m.shape, const),
            pl.BlockSpec((1, mem_len, d), lambda b, t: (b, 0, 0)),
            pl.BlockSpec(wk.shape, const),
            pl.BlockSpec(wv.shape, const),
        ],
        out_specs=out_specs,
        out_shape=out_shape,
        scratch_shapes=[pltpu.VMEM((mem_len, MEM_W), BF16), pltpu.VMEM((mem_len, MEM_W), BF16)],
        compiler_params=pltpu.CompilerParams(
            dimension_semantics=("arbitrary", "arbitrary"), vmem_limit_bytes=VMEM_LIMIT),
        name="mem_attn",
    )(x, w_m, mem, wk, wv)
    return (outs[0], outs[1]) if emit_cast else (outs[0], x)


def _out_ln_body(yh_ref, ya_ref, ym_ref, x_ref, w_ref, g_ref, b_ref, o_ref, ob_ref, *, alpha):
    w = w_ref
    half = x_ref.shape[1] // 2
    halves = [slice(r * half, (r + 1) * half) for r in range(2)]
    ys = [_dot(yh_ref[0, rs], w[:HG_W]) + _dot(ya_ref[0, rs], w[HG_W:HG_W + MLA_W])
          + _dot(ym_ref[0, rs], w[HG_W + MLA_W:]) for rs in halves]
    for rs, y in zip(halves, ys):
        z = alpha * x_ref[0, rs] + y
        mu = jnp.mean(z, axis=-1, keepdims=True)
        zc = z - mu
        var = jnp.mean(zc * zc, axis=-1, keepdims=True)
        out = zc * lax.rsqrt(var + LN_EPS) * g_ref[...] + b_ref[...]
        o_ref[0, rs] = out
        ob_ref[0, rs] = out.astype(ob_ref.dtype)


def _out_ln_call(y_hg, y_mla, y_mem, x, w_out, g, b, *, rows, alpha):
    bsz, seq, d = x.shape
    const = lambda b_, t: (0, 0)
    body = functools.partial(_out_ln_body, alpha=alpha)
    row_blk = lambda width: pl.BlockSpec((1, rows, width), lambda b_, t: (b_, t, 0))
    return pl.pallas_call(
        body,
        grid=(bsz, seq // rows),
        in_specs=[row_blk(HG_W), row_blk(MLA_W), row_blk(MEM_W), row_blk(d),
                  pl.BlockSpec(w_out.shape, const),
                  pl.BlockSpec((1, d), const), pl.BlockSpec((1, d), const)],
        out_specs=[row_blk(d), row_blk(d)],
        out_shape=[jax.ShapeDtypeStruct((bsz, seq, d), F32),
                   jax.ShapeDtypeStruct((bsz, seq, d), BF16)],
        compiler_params=pltpu.CompilerParams(
            dimension_semantics=("arbitrary", "arbitrary"), vmem_limit_bytes=VMEM_LIMIT),
        name="out_ln",
    )(y_hg, y_mla, y_mem, x, w_out, g, b)


def _rotate_half_cols(w):
    half = w.shape[-1] // 2
    return jnp.concatenate([-w[..., half:], w[..., :half]], axis=-1)


def _pad_heads(w, heads, width):
    k = w.shape[0]
    w = w.reshape(k, heads, width)
    w = jnp.pad(w, ((0, 0), (0, 0), (0, MLA_PAD - width)))
    return w.reshape(k, heads * MLA_PAD)


def _mla_weights(w_in_l, w_uq, w_uk, w_uv):
    o = 4 * HG_W
    w_cq = w_in_l[:, o:o + MLA_Q_RANK]
    w_ckv = w_in_l[:, o + MLA_Q_RANK:o + MLA_Q_RANK + MLA_KV_RANK]
    o2 = o + MLA_Q_RANK + MLA_KV_RANK
    w_kr = w_in_l[:, o2:o2 + MLA_ROPE]
    w_gate = w_in_l[:, o2 + MLA_ROPE:o2 + MLA_ROPE + MLA_W]
    lane_pad = lambda w: jnp.pad(w, ((0, 0), (MLA_NOPE, MLA_PAD - MLA_QK)))
    w_b = jnp.concatenate([w_cq, w_ckv, lane_pad(w_kr), lane_pad(_rotate_half_cols(w_kr)), w_gate],
                          axis=1).astype(BF16)

    uq = w_uq.reshape(MLA_Q_RANK, MLA_HEADS, MLA_QK)
    uq_rot = jnp.concatenate([jnp.zeros_like(uq[..., :MLA_NOPE]),
                              _rotate_half_cols(uq[..., MLA_NOPE:])], axis=-1)
    wqa = _pad_heads(uq.reshape(MLA_Q_RANK, -1), MLA_HEADS, MLA_QK).T.astype(BF16)
    wqb = uq_rot[..., MLA_NOPE:].reshape(MLA_Q_RANK, -1).T.astype(BF16)

    uk = w_uk.reshape(MLA_KV_RANK, MLA_HEADS, MLA_NOPE)
    top = jnp.pad(uk, ((0, 0), (0, 0), (0, MLA_PAD - MLA_NOPE)))
    pick = jnp.eye(MLA_PAD, dtype=uk.dtype)
    pick = pick * ((jnp.arange(MLA_PAD) >= MLA_NOPE) & (jnp.arange(MLA_PAD) < MLA_QK))
    bot = jnp.broadcast_to(pick[:, None, :], (MLA_PAD, MLA_HEADS, MLA_PAD))
    wk = jnp.concatenate([top, bot], axis=0).reshape(MLA_KV_RANK + MLA_PAD, -1).astype(BF16)

    wv = w_uv.T.astype(BF16)
    return w_b, wqa, wqb, wk, wv


def _rope_tables(positions):
    inv_freq = 1.0 / (10000.0 ** (jnp.arange(0, MLA_ROPE, 2, dtype=F32) / MLA_ROPE))
    ang = positions.astype(F32)[..., None] * inv_freq
    cos, sin = jnp.cos(ang), jnp.sin(ang)
    return jnp.concatenate([cos, cos], axis=-1), jnp.concatenate([sin, sin], axis=-1)


def kernel(x, mem, positions, w_in, hgrn_lb_logits, hgrn_out_norm, mla_q_norm, mla_kv_norm,
           w_mla_uq, w_mla_uk, w_mla_uv, w_mem_k, w_mem_v, w_out, ln_g, ln_b):
    depth = w_in.shape[0]
    seq = x.shape[1]
    alpha = (2 * depth) ** 0.25
    ctab, stab = _rope_tables(positions)
    ctab_t, stab_t = jnp.swapaxes(ctab, 1, 2), jnp.swapaxes(stab, 1, 2)
    lb_soft = jax.nn.softmax(hgrn_lb_logits.astype(F32), axis=0)
    lower_bounds = jnp.cumsum(lb_soft, axis=0) - lb_soft[0]
    o_mem = 4 * HG_W + MLA_Q_RANK + MLA_KV_RANK + MLA_ROPE + MLA_W
    rows = min(seq, ROW_TILE)
    assert seq % rows == 0

    w_in_b = w_in.astype(BF16)
    xb = x
    for l in range(depth):
        w_l = w_in_b[l]
        w_b, wqa, wqb, wk, wv = _mla_weights(w_l, w_mla_uq[l], w_mla_uk[l], w_mla_uv[l])
        w_m = w_l[:, o_mem:]

        y_mem, xb = _mem_attn_call(xb, w_m, mem, w_mem_k[l].astype(BF16),
                                   w_mem_v[l].astype(BF16), rows=rows)
        y_hg = _hgrn_call(xb, w_in_b, l, lower_bounds[l][None, :], hgrn_out_norm[l][None, :],
                          rows=rows)
        qt, k, vt, gate = _mla_prep_call(xb, w_b, mla_q_norm[l][None, :], mla_kv_norm[l][None, :],
                                         wqa, wqb, wk, wv, ctab, stab, ctab_t, stab_t,
                                         rows=rows, bk=rows // 2)
        y_mla = _mla_attn_call(qt, k, vt, gate, bq=rows)
        x, xb = _out_ln_call(y_hg, y_mla, y_mem, x, w_out[l].astype(BF16),
                             ln_g[l][None, :], ln_b[l][None, :], rows=rows, alpha=alpha)
    return x
```
